```python
import math
import jax, jax.numpy as jnp
from jax import lax
import numpy as np

D_MODEL = 1024
BATCH = 8
SEQ = 4096
DEPTH = 1

D_FF = 2816
EPS = 1e-6
N_HEADS_MLA = 16
QK_NOPE = 64
QK_ROPE = 32
QK_HEAD = QK_NOPE + QK_ROPE
V_HEAD = 64
Q_LORA = 384
KV_LORA = 256
ROPE_BASE = 10000.0
Q_BLOCK = 128
D_INNER = 2 * D_MODEL
SSM_HEAD_DIM = 64
SSM_HEADS = D_INNER // SSM_HEAD_DIM
SSM_GROUPS = 4
D_STATE = 128
CONV_WIDTH = 5
CHUNK = 128
XBC_DIM = D_INNER + 2 * SSM_GROUPS * D_STATE
IN_SPLITS = (Q_LORA, KV_LORA, QK_ROPE, D_INNER, XBC_DIM, SSM_HEADS, SSM_HEADS, D_MODEL, D_MODEL)
IN_DIM = sum(IN_SPLITS)

kernel_name = "hybrid_mla_bissd_macaron_block"


def rmsnorm(x, g):
    xf = x.astype(jnp.float32)
    y = xf * lax.rsqrt(jnp.mean(xf * xf, axis=-1, keepdims=True) + EPS)
    return (y * g).astype(x.dtype)


def swiglu(h, w_gate, w_up, w_down):
    return (jax.nn.silu(h @ w_gate) * (h @ w_up)) @ w_down


def rope_tables(positions):
    inv_freq = 1.0 / (ROPE_BASE ** (jnp.arange(0, QK_ROPE, 2, dtype=jnp.float32) / QK_ROPE))
    ang = positions.astype(jnp.float32)[..., None] * inv_freq
    return jnp.cos(ang)[:, :, None, :], jnp.sin(ang)[:, :, None, :]


def apply_rope(t, cos, sin):
    tf = t.astype(jnp.float32)
    t1, t2 = tf[..., : QK_ROPE // 2], tf[..., QK_ROPE // 2:]
    return jnp.concatenate([t1 * cos - t2 * sin, t2 * cos + t1 * sin], axis=-1).astype(t.dtype)


def mla(c_q, c_kv, k_pe, positions, q_a_norm, w_q_b, kv_a_norm, w_kv_b, q_head_norm, k_head_norm):
    b, s, _ = c_q.shape
    q = (rmsnorm(c_q, q_a_norm) @ w_q_b).reshape(b, s, N_HEADS_MLA, QK_HEAD)
    kv = (rmsnorm(c_kv, kv_a_norm) @ w_kv_b).reshape(b, s, N_HEADS_MLA, QK_NOPE + V_HEAD)
    k_nope, v = kv[..., :QK_NOPE], kv[..., QK_NOPE:]
    k_pe_h = jnp.broadcast_to(k_pe[:, :, None, :], (b, s, N_HEADS_MLA, QK_ROPE))
    k = jnp.concatenate([k_nope, k_pe_h], axis=-1)
    q = rmsnorm(q, q_head_norm)
    k = rmsnorm(k, k_head_norm)
    cos, sin = rope_tables(positions)
    q = jnp.concatenate([q[..., :QK_NOPE], apply_rope(q[..., QK_NOPE:], cos, sin)], axis=-1)
    k = jnp.concatenate([k[..., :QK_NOPE], apply_rope(k[..., QK_NOPE:], cos, sin)], axis=-1)
    scale = 1.0 / math.sqrt(QK_HEAD)
    n_blk = s // Q_BLOCK
    qb = q.reshape(b, n_blk, Q_BLOCK, N_HEADS_MLA, QK_HEAD).transpose(1, 0, 2, 3, 4)

    def attend(q_blk):
        sc = jnp.einsum('bqhd,bkhd->bhqk', q_blk, k).astype(jnp.float32) * scale
        p = jax.nn.softmax(sc, axis=-1).astype(v.dtype)
        return jnp.einsum('bhqk,bkhd->bqhd', p, v)

    o = lax.map(attend, qb)
    return o.transpose(1, 0, 2, 3, 4).reshape(b, s, N_HEADS_MLA * V_HEAD)


def ssd(x, dt, a, bm, cm):
    b, l, h, p = x.shape
    g, n = bm.shape[2], bm.shape[3]
    hg = h // g
    nc = l // CHUNK
    f32 = jnp.float32
    xdt = (x.astype(f32) * dt[..., None]).reshape(b, nc, CHUNK, g, hg, p)
    da = jnp.moveaxis((dt * a).reshape(b, nc, CHUNK, g, hg), 2, -1)
    a_cs = jnp.cumsum(da, axis=-1)
    bc = bm.astype(f32).reshape(b, nc, CHUNK, g, n)
    cc = cm.astype(f32).reshape(b, nc, CHUNK, g, n)
    tril = jnp.tril(jnp.ones((CHUNK, CHUNK), dtype=bool))
    seg = a_cs[..., :, None] - a_cs[..., None, :]
    decay = jnp.exp(jnp.where(tril, seg, -jnp.inf))
    cb = jnp.einsum('bclgn,bcsgn->bcgls', cc, bc)
    y_diag = jnp.einsum('bcgls,bcghls,bcsghp->bclghp', cb, decay, xdt)
    decay_states = jnp.exp(a_cs[..., -1:] - a_cs)
    states = jnp.einsum('bclgn,bcghl,bclghp->bcghpn', bc, decay_states, xdt)
    chunk_decay = jnp.exp(a_cs[..., -1])

    def step(carry, inp):
        st, dec = inp
        return carry * dec[..., None, None] + st, carry

    init = jnp.zeros((b, g, hg, p, n), f32)
    _, prev = lax.scan(step, init, (jnp.moveaxis(states, 1, 0), jnp.moveaxis(chunk_decay, 1, 0)))
    prev = jnp.moveaxis(prev, 0, 1)
    y_off = jnp.einsum('bclgn,bcghpn,bcghl->bclghp', cc, prev, jnp.exp(a_cs))
    return (y_diag + y_off).reshape(b, l, h, p)


def bi_mamba2(xbc, z, dt_f_raw, dt_b_raw, conv_w, conv_b, a_log_fwd, a_log_bwd,
              dt_bias_fwd, dt_bias_bwd, d_skip, ssm_norm):
    b, s, _ = xbc.shape
    pad = CONV_WIDTH // 2
    xbc = lax.conv_general_dilated(xbc, conv_w, window_strides=(1,), padding=[(pad, pad)],
                                   dimension_numbers=('NWC', 'WIO', 'NWC'),
                                   feature_group_count=XBC_DIM)
    xbc = jax.nn.silu(xbc + conv_b)
    xs, bm, cm = jnp.split(xbc, [D_INNER, D_INNER + SSM_GROUPS * D_STATE], axis=-1)
    xs = xs.reshape(b, s, SSM_HEADS, SSM_HEAD_DIM)
    bm = bm.reshape(b, s, SSM_GROUPS, D_STATE)
    cm = cm.reshape(b, s, SSM_GROUPS, D_STATE)
    dt_f = jax.nn.softplus(dt_f_raw.astype(jnp.float32) + dt_bias_fwd)
    dt_b = jax.nn.softplus(dt_b_raw.astype(jnp.float32) + dt_bias_bwd)
    a_f = -jnp.exp(a_log_fwd.astype(jnp.float32))
    a_b = -jnp.exp(a_log_bwd.astype(jnp.float32))
    rev = lambda t: jnp.flip(t, axis=1)
    y_f = ssd(xs, dt_f, a_f, bm, cm)
    y_b = rev(ssd(rev(xs), rev(dt_b), a_b, rev(bm), rev(cm)))
    y = y_f + y_b + d_skip.astype(jnp.float32)[:, None] * xs.astype(jnp.float32)
    y = y.reshape(b, s, D_INNER) * jax.nn.silu(z.astype(jnp.float32))
    yg = y.reshape(b, s, SSM_GROUPS, D_INNER // SSM_GROUPS)
    yg = yg * lax.rsqrt(jnp.mean(yg * yg, axis=-1, keepdims=True) + EPS)
    return (yg.reshape(b, s, D_INNER) * ssm_norm).astype(xs.dtype)


def setup_inputs(seed: int = 0) -> dict:
    key = jax.random.key(seed)
    ks = iter(jax.random.split(key, 40))
    f32 = jnp.float32

    def nrm(shape, scale):
        return jax.random.normal(next(ks), (DEPTH,) + shape, f32) * scale

    def gain(n):
        return 1.0 + 0.01 * jax.random.normal(next(ks), (DEPTH, n), f32)

    x = jax.random.normal(next(ks), (BATCH, SEQ, D_MODEL), f32)
    positions = jnp.broadcast_to(jnp.arange(SEQ, dtype=jnp.int32)[None, :], (BATCH, SEQ))
    d = {}
    d['x'] = x
    d['positions'] = positions
    d['ffn1_norm'] = gain(D_MODEL)
    d['ffn1_w_gate'] = nrm((D_MODEL, D_FF), D_MODEL ** -0.5)
    d['ffn1_w_up'] = nrm((D_MODEL, D_FF), D_MODEL ** -0.5)
    d['ffn1_w_down'] = nrm((D_FF, D_MODEL), D_FF ** -0.5)
    d['mix_norm'] = gain(D_MODEL)
    d['w_in'] = nrm((D_MODEL, IN_DIM), D_MODEL ** -0.5)
    d['q_a_norm'] = gain(Q_LORA)
    d['w_q_b'] = nrm((Q_LORA, N_HEADS_MLA * QK_HEAD), Q_LORA ** -0.5)
    d['kv_a_norm'] = gain(KV_LORA)
    d['w_kv_b'] = nrm((KV_LORA, N_HEADS_MLA * (QK_NOPE + V_HEAD)), KV_LORA ** -0.5)
    d['q_head_norm'] = gain(QK_HEAD)
    d['k_head_norm'] = gain(QK_HEAD)
    d['conv_w'] = nrm((CONV_WIDTH, 1, XBC_DIM), CONV_WIDTH ** -0.5)
    d['conv_b'] = nrm((XBC_DIM,), 0.01)
    d['a_log_fwd'] = jnp.log(jax.random.uniform(next(ks), (DEPTH, SSM_HEADS), f32, 1.0, 16.0))
    d['a_log_bwd'] = jnp.log(jax.random.uniform(next(ks), (DEPTH, SSM_HEADS), f32, 1.0, 16.0))
    dt0_f = jnp.exp(jax.random.uniform(next(ks), (DEPTH, SSM_HEADS), f32, math.log(1e-3), math.log(1e-1)))
    dt0_b = jnp.exp(jax.random.uniform(next(ks), (DEPTH, SSM_HEADS), f32, math.log(1e-3), math.log(1e-1)))
    d['dt_bias_fwd'] = dt0_f + jnp.log(-jnp.expm1(-dt0_f))
    d['dt_bias_bwd'] = dt0_b + jnp.log(-jnp.expm1(-dt0_b))
    d['d_skip'] = gain(SSM_HEADS)
    d['ssm_norm'] = gain(D_INNER)
    d['w_attn_branch'] = nrm((N_HEADS_MLA * V_HEAD, D_MODEL), (N_HEADS_MLA * V_HEAD) ** -0.5)
    d['w_ssm_branch'] = nrm((D_INNER, D_MODEL), D_INNER ** -0.5)
    d['w_out'] = nrm((D_MODEL, D_MODEL), D_MODEL ** -0.5)
    d['ffn2_norm'] = gain(D_MODEL)
    d['ffn2_w_gate'] = nrm((D_MODEL, D_FF), D_MODEL ** -0.5)
    d['ffn2_w_up'] = nrm((D_MODEL, D_FF), D_MODEL ** -0.5)
    d['ffn2_w_down'] = nrm((D_FF, D_MODEL), D_FF ** -0.5)
    return d


def reference(x, positions, ffn1_norm, ffn1_w_gate, ffn1_w_up, ffn1_w_down, mix_norm, w_in,
              q_a_norm, w_q_b, kv_a_norm, w_kv_b, q_head_norm, k_head_norm,
              conv_w, conv_b, a_log_fwd, a_log_bwd, dt_bias_fwd, dt_bias_bwd, d_skip, ssm_norm,
              w_attn_branch, w_ssm_branch, w_out,
              ffn2_norm, ffn2_w_gate, ffn2_w_up, ffn2_w_down):
    split_idx = list(np.cumsum(IN_SPLITS)[:-1])
    for l in range(DEPTH):
        x = x + 0.5 * swiglu(rmsnorm(x, ffn1_norm[l]), ffn1_w_gate[l], ffn1_w_up[l], ffn1_w_down[l])
        h = rmsnorm(x, mix_norm[l])
        u = h @ w_in[l]
        c_q, c_kv, k_pe, z, xbc, dt_f, dt_b, g_a, g_b = jnp.split(u, split_idx, axis=-1)
        a = mla(c_q, c_kv, k_pe, positions, q_a_norm[l], w_q_b[l], kv_a_norm[l], w_kv_b[l],
                q_head_norm[l], k_head_norm[l])
        m = bi_mamba2(xbc, z, dt_f, dt_b, conv_w[l], conv_b[l], a_log_fwd[l], a_log_bwd[l],
                      dt_bias_fwd[l], dt_bias_bwd[l], d_skip[l], ssm_norm[l])
        merged = jax.nn.sigmoid(g_a) * (a @ w_attn_branch[l]) + jax.nn.sigmoid(g_b) * (m @ w_ssm_branch[l])
        x = x + merged @ w_out[l]
        x = x + 0.5 * swiglu(rmsnorm(x, ffn2_norm[l]), ffn2_w_gate[l], ffn2_w_up[l], ffn2_w_down[l])
    return x
```

```python
import functools
import math

import numpy as np
import jax
import jax.numpy as jnp
from jax import lax
from jax.experimental import pallas as pl
from jax.experimental.pallas import tpu as pltpu

F32 = jnp.float32
BF16 = jnp.bfloat16

D_MODEL = 1024
D_FF = 2816
EPS = 1e-6
N_HEADS_MLA = 16
QK_NOPE = 64
QK_ROPE = 32
QK_HEAD = QK_NOPE + QK_ROPE
V_HEAD = 64
Q_LORA = 384
KV_LORA = 256
ROPE_BASE = 10000.0
D_INNER = 2 * D_MODEL
SSM_HEAD_DIM = 64
SSM_HEADS = D_INNER // SSM_HEAD_DIM
SSM_GROUPS = 4
HEADS_PER_GROUP = SSM_HEADS // SSM_GROUPS
D_STATE = 128
CONV_WIDTH = 5
CHUNK = 128
XBC_DIM = D_INNER + 2 * SSM_GROUPS * D_STATE
GROUP_DIM = D_INNER // SSM_GROUPS

LANES = 128
HEAD_SLOT = LANES
VMEM_LIMIT_BYTES = 56 * 1024 * 1024


def _params(sem, vmem=VMEM_LIMIT_BYTES):
    return pltpu.CompilerParams(dimension_semantics=sem, vmem_limit_bytes=vmem)


def _const_spec(shape):
    nd = len(shape)
    return pl.BlockSpec(shape, lambda *_: (0,) * nd, pipeline_mode=pl.Buffered(1))


def _rms(x, g):
    return x * lax.rsqrt(jnp.mean(x * x, axis=-1, keepdims=True) + EPS) * g


def _silu(x):
    return x * jax.nn.sigmoid(x)


def _dot(a, b):
    return jnp.dot(a, b, preferred_element_type=F32)


def _dot_split(a, b):
    hi = a.astype(BF16)
    lo = (a - hi.astype(F32)).astype(BF16)
    return _dot(hi, b) + _dot(lo, b)


def _ffn_kernel(x_ref, g_ref, wg_ref, wu_ref, wd_ref, o_ref):
    x = x_ref[0]
    hb = _rms(x, g_ref[...]).astype(BF16)
    gate = _dot(hb, wg_ref[...])
    up = _dot(hb, wu_ref[...])
    act = (_silu(gate) * up).astype(BF16)
    o_ref[0] = x + 0.5 * _dot(act, wd_ref[...])


def _ffn(x, g, wg, wu, wd, tm):
    b, s, d = x.shape
    tok = pl.BlockSpec((1, tm, d), lambda i, j: (i, j, 0))
    return pl.pallas_call(
        _ffn_kernel,
        out_shape=jax.ShapeDtypeStruct(x.shape, F32),
        grid=(b, s // tm),
        in_specs=[tok, _const_spec(g.shape), _const_spec(wg.shape), _const_spec(wu.shape),
                  _const_spec(wd.shape)],
        out_specs=tok,
        compiler_params=_params(("parallel", "parallel")),
        name="ffn",
    )(x, g, wg, wu, wd)


def _inproj_kernel(x_ref, g_ref, wcq, wckv, wkpe, wz, wxbc, wdt, wga, wgb,
                   ocq, ockv, okpe, oz, oxbc, odt, oga, ogb):
    hb = _rms(x_ref[0], g_ref[...]).astype(BF16)
    ocq[0] = _dot(hb, wcq[...])
    ockv[0] = _dot(hb, wckv[...])
    okpe[0] = _dot(hb, wkpe[...])
    oz[0] = _dot(hb, wz[...]).astype(BF16)
    oxbc[0] = _dot(hb, wxbc[...]).astype(BF16)
    odt[0] = _dot(hb, wdt[...])
    oga[0] = _dot(hb, wga[...]).astype(BF16)
    ogb[0] = _dot(hb, wgb[...]).astype(BF16)


def _inproj(x, g, ws, tm):
    b, s, d = x.shape
    dts = (F32, F32, F32, BF16, BF16, F32, BF16, BF16)
    tok = lambda n: pl.BlockSpec((1, tm, n), lambda i, j: (i, j, 0))
    return pl.pallas_call(
        _inproj_kernel,
        out_shape=[jax.ShapeDtypeStruct((b, s, w.shape[1]), dt) for w, dt in zip(ws, dts)],
        grid=(b, s // tm),
        in_specs=[tok(d), _const_spec(g.shape)] + [_const_spec(w.shape) for w in ws],
        out_specs=[tok(w.shape[1]) for w in ws],
        compiler_params=_params(("parallel", "parallel")),
        name="inproj",
    )(x, g, *ws)


def _mlaprep_kernel(cq_ref, ckv_ref, kpe_ref, pos_ref, qan, wq, kvan, wk, wv, qhn, khn, invf, sgn,
                    oq, ok, ov):
    qn = _rms(cq_ref[0], qan[...]).astype(BF16)
    kvn = _rms(ckv_ref[0], kvan[...]).astype(BF16)
    q_full = _dot(qn, wq[...])
    k_full = _dot(kvn, wk[...])
    ov[0] = _dot(kvn, wv[...]).astype(BF16)
    kpe = kpe_ref[0]
    ang = pos_ref[0].astype(F32) * invf[...]
    cos = jnp.cos(ang)
    sin = jnp.sin(ang) * sgn[...]
    lane = lax.broadcasted_iota(jnp.int32, cos.shape, 1)
    first_half = lane < QK_NOPE + QK_ROPE // 2

    def norm_rope(t, gain):
        t = t * lax.rsqrt(jnp.sum(t * t, axis=-1, keepdims=True) * (1.0 / QK_HEAD) + EPS) * gain
        partner = jnp.where(first_half,
                            pltpu.roll(t, LANES - QK_ROPE // 2, axis=1),
                            pltpu.roll(t, QK_ROPE // 2, axis=1))
        return t * cos + partner * sin

    for h in range(N_HEADS_MLA):
        sl = slice(h * HEAD_SLOT, (h + 1) * HEAD_SLOT)
        oq[0, :, sl] = norm_rope(q_full[:, sl], qhn[...]).astype(BF16)
        ok[0, :, sl] = norm_rope(k_full[:, sl] + kpe, khn[...]).astype(BF16)


def _mlaprep(cq, ckv, kpe, pos, consts, tm):
    b, s, _ = cq.shape
    tok = lambda n: pl.BlockSpec((1, tm, n), lambda i, j: (i, j, 0))
    hw = N_HEADS_MLA * HEAD_SLOT
    return pl.pallas_call(
        _mlaprep_kernel,
        out_shape=[jax.ShapeDtypeStruct((b, s, hw), BF16), jax.ShapeDtypeStruct((b, s, hw), BF16),
                   jax.ShapeDtypeStruct((b, s, N_HEADS_MLA * V_HEAD), BF16)],
        grid=(b, s // tm),
        in_specs=[tok(Q_LORA), tok(KV_LORA), tok(LANES), tok(1)] + [_const_spec(c.shape) for c in consts],
        out_specs=[tok(hw), tok(hw), tok(N_HEADS_MLA * V_HEAD)],
        compiler_params=_params(("parallel", "parallel")),
        name="mlaprep",
    )(cq, ckv, kpe, pos, *consts)


def _flash_kernel(q_ref, k_ref, v_ref, o_ref, *, tk):
    tq = q_ref.shape[1]
    nk = k_ref.shape[1] // tk
    outs = []
    for h in range(2):
        sl = slice(h * HEAD_SLOT, (h + 1) * HEAD_SLOT)
        q = q_ref[0, :, sl]

        def body(j, carry, sl=sl, q=q):
            m, l, acc = carry
            rows = pl.ds(pl.multiple_of(j * tk, tk), tk)
            k = k_ref[0, rows, sl]
            v = v_ref[0, rows, :]
            sc = lax.dot_general(q, k, (((1,), (1,)), ((), ())), preferred_element_type=F32)
            m_new = jnp.maximum(m, jnp.max(sc, axis=-1, keepdims=True))
            alpha = jnp.exp(m - m_new)
            p = jnp.exp(sc - m_new)
            l = alpha * l + jnp.sum(p, axis=-1, keepdims=True)
            acc = alpha * acc + _dot(p.astype(BF16), v)
            return m_new, l, acc

        init = (jnp.full((tq, 1), -jnp.inf, F32), jnp.zeros((tq, 1), F32),
                jnp.zeros((tq, 2 * V_HEAD), F32))
        _, l, acc = lax.fori_loop(0, nk, body, init)
        outs.append(acc / l)
    lane = lax.broadcasted_iota(jnp.int32, outs[0].shape, 1)
    o_ref[0] = jnp.where(lane < V_HEAD, outs[0], outs[1]).astype(o_ref.dtype)


def _flash(q, k, v, tq, tk):
    b, s, _ = q.shape
    return pl.pallas_call(
        functools.partial(_flash_kernel, tk=tk),
        out_shape=jax.ShapeDtypeStruct(v.shape, BF16),
        grid=(b, N_HEADS_MLA // 2, s // tq),
        in_specs=[pl.BlockSpec((1, tq, 2 * HEAD_SLOT), lambda i, p, j: (i, j, p)),
                  pl.BlockSpec((1, s, 2 * HEAD_SLOT), lambda i, p, j: (i, 0, p)),
                  pl.BlockSpec((1, s, 2 * V_HEAD), lambda i, p, j: (i, 0, p))],
        out_specs=pl.BlockSpec((1, tq, 2 * V_HEAD), lambda i, p, j: (i, j, p)),
        compiler_params=_params(("parallel", "parallel", "arbitrary")),
        name="flash",
    )(q, k, v)


CONV_HALO = 8


def _conv_kernel(x_ref, w_ref, b_ref, o_ref, pad_ref, *, rows):
    s = x_ref.shape[1]
    zeros = jnp.zeros((CONV_HALO, pad_ref.shape[1]), F32)
    pad_ref[0:CONV_HALO, :] = zeros
    pad_ref[CONV_HALO + s:2 * CONV_HALO + s, :] = zeros
    for c in range(s // rows):
        pad_ref[CONV_HALO + c * rows:CONV_HALO + (c + 1) * rows, :] = (
            x_ref[0, c * rows:(c + 1) * rows, :].astype(F32))
    w = w_ref[...]
    for c in range(s // rows):
        acc = jnp.broadcast_to(b_ref[...], (rows, pad_ref.shape[1]))
        for t in range(CONV_WIDTH):
            lo = CONV_HALO + c * rows + t - CONV_WIDTH // 2
            acc = acc + w[t:t + 1, :] * pad_ref[lo:lo + rows, :]
        o_ref[0, c * rows:(c + 1) * rows, :] = _silu(acc).astype(o_ref.dtype)


def _conv(xbc, w, bias, rows):
    b, s, n = xbc.shape
    return pl.pallas_call(
        functools.partial(_conv_kernel, rows=rows),
        out_shape=jax.ShapeDtypeStruct(xbc.shape, BF16),
        grid=(b, n // GROUP_DIM),
        in_specs=[pl.BlockSpec((1, s, GROUP_DIM), lambda i, c: (i, 0, c)),
                  pl.BlockSpec((CONV_WIDTH, GROUP_DIM), lambda i, c: (0, c)),
                  pl.BlockSpec((1, GROUP_DIM), lambda i, c: (0, c))],
        out_specs=pl.BlockSpec((1, s, GROUP_DIM), lambda i, c: (i, 0, c)),
        scratch_shapes=[pltpu.VMEM((s + 2 * CONV_HALO, GROUP_DIM), F32)],
        compiler_params=_params(("parallel", "parallel")),
        name="conv",
    )(xbc, w, bias)


def _ssd_kernel(xs_ref, bm_ref, cm_ref, z_ref, dt_ref, bias_ref, alog_ref, dskip_ref, norm_ref,
                ef_ref, eb_ref, o_ref, y_scr, exp_scr, w_scr, sf_scr, sb_scr):
    L = CHUNK
    hpg = HEADS_PER_GROUP
    nc = xs_ref.shape[1] // L
    a = -jnp.exp(alog_ref[0])
    bias = bias_ref[0]
    ef = ef_ref[...]
    eb = eb_ref[...]
    li = lax.broadcasted_iota(jnp.int32, (L, L), 0)
    si = lax.broadcasted_iota(jnp.int32, (L, L), 1)
    tril = (si <= li).astype(BF16)
    triu = (si >= li).astype(BF16)
    lane = lax.broadcasted_iota(jnp.int32, (L, LANES), 1)
    is_fwd_lane = lane < hpg
    half = lane < SSM_HEAD_DIM

    sf_scr[...] = jnp.zeros_like(sf_scr)
    sb_scr[...] = jnp.zeros_like(sb_scr)

    def fwd_chunk(c, carry):
        rows = pl.ds(pl.multiple_of(c * L, L), L)
        dtv = jax.nn.softplus(dt_ref[0, rows, :] + bias)
        da = dtv * a
        cs = jnp.where(is_fwd_lane, _dot_split_left(tril, da), _dot_split_left(triu, da))
        tot = jnp.where(is_fwd_lane[0:1], cs[L - 1:L, :], cs[0:1, :])
        expc = jnp.exp(cs)
        wst = jnp.exp(tot - cs) * dtv
        exp_scr[rows, :] = expc
        w_scr[rows, :] = wst
        cs_t = cs.T
        dt_t = dtv.T
        xs = xs_ref[0, rows, :]
        bmat = bm_ref[0, rows, :]
        cmat = cm_ref[0, rows, :]
        cbm = lax.dot_general(cmat, bmat, (((1,), (1,)), ((), ())), preferred_element_type=F32)
        pairs = []
        for jp in range(hpg // 2):
            xp = xs[:, jp * LANES:(jp + 1) * LANES]
            ys = []
            for j in (2 * jp, 2 * jp + 1):
                arg = jnp.where(si <= li,
                                cs[:, j:j + 1] - cs_t[j:j + 1, :],
                                cs[:, hpg + j:hpg + j + 1] - cs_t[hpg + j:hpg + j + 1, :])
                dtf = dt_t[j:j + 1, :]
                dtb = dt_t[hpg + j:hpg + j + 1, :]
                mult = jnp.where(si < li, dtf, jnp.where(si > li, dtb, dtf + dtb))
                wmat = (cbm * jnp.exp(arg) * mult).astype(BF16)
                ys.append(_dot(wmat, xp))
            pairs.append(jnp.where(half, ys[0], ys[1]))
        y = jnp.concatenate(pairs, axis=1)
        y = y + _dot(cmat, sf_scr[...].astype(BF16)) * _dot_split(expc, ef)
        y_scr[rows, :] = y
        xw = (xs.astype(F32) * _dot_split(wst, ef)).astype(BF16)
        ds = lax.dot_general(bmat, xw, (((0,), (0,)), ((), ())), preferred_element_type=F32)
        sf_scr[...] = sf_scr[...] * _dot_split(expc[L - 1:L, :], ef) + ds
        return carry

    lax.fori_loop(0, nc, fwd_chunk, 0)

    dskip = dskip_ref[...]
    gnorm = norm_ref[...]

    def bwd_chunk(i, carry):
        c = nc - 1 - i
        rows = pl.ds(pl.multiple_of(c * L, L), L)
        expc = exp_scr[rows, :]
        wst = w_scr[rows, :]
        xs = xs_ref[0, rows, :]
        bmat = bm_ref[0, rows, :]
        cmat = cm_ref[0, rows, :]
        xf = xs.astype(F32)
        y = y_scr[rows, :] + _dot(cmat, sb_scr[...].astype(BF16)) * _dot_split(expc, eb)
        xw = (xf * _dot_split(wst, eb)).astype(BF16)
        ds = lax.dot_general(bmat, xw, (((0,), (0,)), ((), ())), preferred_element_type=F32)
        sb_scr[...] = sb_scr[...] * _dot_split(expc[0:1, :], eb) + ds
        y = (y + dskip * xf) * _silu(z_ref[0, rows, :].astype(F32))
        y = y * lax.rsqrt(jnp.mean(y * y, axis=-1, keepdims=True) + EPS) * gnorm
        o_ref[0, rows, :] = y.astype(o_ref.dtype)
        return carry

    lax.fori_loop(0, nc, bwd_chunk, 0)


def _dot_split_left(t, a):
    hi = a.astype(BF16)
    lo = (a - hi.astype(F32)).astype(BF16)
    return _dot(t, hi) + _dot(t, lo)


def _ssd(xc, z, dt, bias, alog, dskip, gnorm, ef, eb):
    b, s, _ = xc.shape
    nb = D_INNER // LANES
    grp = lambda n: pl.BlockSpec((1, s, n), lambda i, g: (i, 0, g))
    return pl.pallas_call(
        _ssd_kernel,
        out_shape=jax.ShapeDtypeStruct((b, s, D_INNER), BF16),
        grid=(b, SSM_GROUPS),
        in_specs=[grp(GROUP_DIM),
                  pl.BlockSpec((1, s, D_STATE), lambda i, g: (i, 0, nb + g)),
                  pl.BlockSpec((1, s, D_STATE), lambda i, g: (i, 0, nb + SSM_GROUPS + g)),
                  grp(GROUP_DIM), grp(LANES),
                  pl.BlockSpec((1, 1, LANES), lambda i, g: (g, 0, 0)),
                  pl.BlockSpec((1, 1, LANES), lambda i, g: (g, 0, 0)),
                  pl.BlockSpec((1, GROUP_DIM), lambda i, g: (0, g)),
                  pl.BlockSpec((1, GROUP_DIM), lambda i, g: (0, g)),
                  _const_spec(ef.shape), _const_spec(eb.shape)],
        out_specs=grp(GROUP_DIM),
        scratch_shapes=[pltpu.VMEM((s, GROUP_DIM), F32), pltpu.VMEM((s, LANES), F32),
                        pltpu.VMEM((s, LANES), F32), pltpu.VMEM((D_STATE, GROUP_DIM), F32),
                        pltpu.VMEM((D_STATE, GROUP_DIM), F32)],
        compiler_params=_params(("parallel", "parallel")),
        name="ssd",
    )(xc, xc, xc, z, dt, bias, alog, dskip, gnorm, ef, eb)


def _merge_kernel(x_ref, a_ref, m_ref, ga_ref, gb_ref, pa, pb, wo, o_ref):
    ba = _dot(a_ref[0], pa[...])
    bm = _dot(m_ref[0], pb[...])
    merged = (jax.nn.sigmoid(ga_ref[0].astype(F32)) * ba
              + jax.nn.sigmoid(gb_ref[0].astype(F32)) * bm)
    o_ref[0] = x_ref[0] + _dot(merged.astype(BF16), wo[...])


def _merge(x, a, m, ga, gb, pa, pb, wo, tm):
    b, s, d = x.shape
    tok = lambda n: pl.BlockSpec((1, tm, n), lambda i, j: (i, j, 0))
    return pl.pallas_call(
        _merge_kernel,
        out_shape=jax.ShapeDtypeStruct(x.shape, F32),
        grid=(b, s // tm),
        in_specs=[tok(d), tok(a.shape[2]), tok(m.shape[2]), tok(d), tok(d),
                  _const_spec(pa.shape), _const_spec(pb.shape), _const_spec(wo.shape)],
        out_specs=tok(d),
        compiler_params=_params(("parallel", "parallel")),
        name="merge",
    )(x, a, m, ga, gb, pa, pb, wo)


def _pad_heads(w, width):
    k = w.shape[0]
    w = w.reshape(k, N_HEADS_MLA, width)
    return jnp.pad(w, ((0, 0), (0, 0), (0, HEAD_SLOT - width))).reshape(k, N_HEADS_MLA * HEAD_SLOT)


def _group_lanes(v_f, v_b, fill):
    f = v_f.reshape(SSM_GROUPS, HEADS_PER_GROUP)
    bk = v_b.reshape(SSM_GROUPS, HEADS_PER_GROUP)
    rest = jnp.full((SSM_GROUPS, LANES - 2 * HEADS_PER_GROUP), fill, F32)
    return jnp.concatenate([f, bk, rest], axis=1).reshape(SSM_GROUPS, 1, LANES)


def _expanders():
    ch = np.arange(GROUP_DIM) // SSM_HEAD_DIM
    row = np.arange(LANES)[:, None]
    ef = (row == ch[None, :]).astype(np.float32)
    eb = (row == ch[None, :] + HEADS_PER_GROUP).astype(np.float32)
    return jnp.asarray(ef, BF16), jnp.asarray(eb, BF16)


def _rope_lanes():
    inv_freq = 1.0 / (ROPE_BASE ** (jnp.arange(0, QK_ROPE, 2, dtype=F32) / QK_ROPE))
    half = QK_ROPE // 2
    zeros = jnp.zeros((QK_NOPE,), F32)
    tail = jnp.zeros((HEAD_SLOT - QK_HEAD,), F32)
    invf = jnp.concatenate([zeros, inv_freq, inv_freq, tail]).reshape(1, HEAD_SLOT)
    sgn = jnp.concatenate([zeros, -jnp.ones((half,), F32), jnp.ones((half,), F32), tail])
    return invf, sgn.reshape(1, HEAD_SLOT)


def kernel(x, positions, ffn1_norm, ffn1_w_gate, ffn1_w_up, ffn1_w_down, mix_norm, w_in, q_a_norm, w_q_b, kv_a_norm, w_kv_b, q_head_norm, k_head_norm, conv_w, conv_b, a_log_fwd, a_log_bwd, dt_bias_fwd, dt_bias_bwd, d_skip, ssm_norm, w_attn_branch, w_ssm_branch, w_out, ffn2_norm, ffn2_w_gate, ffn2_w_up, ffn2_w_down):
    b, s, d = x.shape
    depth = ffn1_norm.shape[0]
    tm = min(512, s)
    bf = lambda w: w.astype(BF16)
    row = lambda v: v.reshape(1, -1).astype(F32)
    ef, eb = _expanders()
    invf, sgn = _rope_lanes()
    pos = positions.reshape(b, s, 1).astype(jnp.int32)

    for l in range(depth):
        x = _ffn(x, row(ffn1_norm[l]), bf(ffn1_w_gate[l]), bf(ffn1_w_up[l]), bf(ffn1_w_down[l]), tm)

        w = w_in[l]
        o = np.cumsum((0, Q_LORA, KV_LORA, QK_ROPE, D_INNER, XBC_DIM, SSM_HEADS, SSM_HEADS,
                       D_MODEL, D_MODEL))
        seg = [w[:, o[i]:o[i + 1]] for i in range(9)]
        w_kpe = jnp.pad(seg[2], ((0, 0), (QK_NOPE, HEAD_SLOT - QK_HEAD)))
        dtf = seg[5].reshape(d, SSM_GROUPS, HEADS_PER_GROUP)
        dtb = seg[6].reshape(d, SSM_GROUPS, HEADS_PER_GROUP)
        w_dt = jnp.pad(jnp.concatenate([dtf, dtb], axis=2),
                       ((0, 0), (0, 0), (0, LANES - 2 * HEADS_PER_GROUP))).reshape(d, SSM_GROUPS * LANES)
        ws = [bf(seg[0]), bf(seg[1]), bf(w_kpe), bf(seg[3]), bf(seg[4]), bf(w_dt), bf(seg[7]), bf(seg[8])]
        cq, ckv, kpe, z, xbc, dt, ga, gb = _inproj(x, row(mix_norm[l]), ws, tm)

        scale = 1.0 / math.sqrt(QK_HEAD)
        wkv = w_kv_b[l].reshape(KV_LORA, N_HEADS_MLA, QK_NOPE + V_HEAD)
        w_k = _pad_heads(wkv[:, :, :QK_NOPE].reshape(KV_LORA, N_HEADS_MLA * QK_NOPE), QK_NOPE)
        w_v = wkv[:, :, QK_NOPE:].reshape(KV_LORA, N_HEADS_MLA * V_HEAD)
        pad_gain = lambda g: jnp.pad(g, (0, HEAD_SLOT - QK_HEAD)).reshape(1, HEAD_SLOT)
        consts = [row(q_a_norm[l]), bf(_pad_heads(w_q_b[l], QK_HEAD)), row(kv_a_norm[l]), bf(w_k), bf(w_v),
                  pad_gain(q_head_norm[l] * scale), pad_gain(k_head_norm[l]), invf, sgn]
        q, k, v = _mlaprep(cq, ckv, kpe, pos, consts, tm)
        a = _flash(q, k, v, min(512, s), min(512, s))

        xc = _conv(xbc, conv_w[l].reshape(CONV_WIDTH, XBC_DIM), row(conv_b[l]), min(512, s))
        m = _ssd(xc, z, dt,
                 _group_lanes(dt_bias_fwd[l], dt_bias_bwd[l], 0.0),
                 _group_lanes(a_log_fwd[l], a_log_bwd[l], 0.0),
                 row(jnp.repeat(d_skip[l], SSM_HEAD_DIM)), row(ssm_norm[l]), ef, eb)

        x = _merge(x, a, m, ga, gb, bf(w_attn_branch[l]), bf(w_ssm_branch[l]), bf(w_out[l]), tm)
        x = _ffn(x, row(ffn2_norm[l]), bf(ffn2_w_gate[l]), bf(ffn2_w_up[l]), bf(ffn2_w_down[l]), tm)
    return x
```

```python
import functools
import math

import numpy as np
import jax
import jax.numpy as jnp
from jax import lax
from jax.experimental import pallas as pl
from jax.experimental.pallas import tpu as pltpu

F32 = jnp.float32
BF16 = jnp.bfloat16

D_MODEL = 1024
D_FF = 2816
EPS = 1e-6
N_HEADS_MLA = 16
QK_NOPE = 64
QK_ROPE = 32
QK_HEAD = QK_NOPE + QK_ROPE
V_HEAD = 64
Q_LORA = 384
KV_LORA = 256
ROPE_BASE = 10000.0
D_INNER = 2 * D_MODEL
SSM_HEAD_DIM = 64
SSM_HEADS = D_INNER // SSM_HEAD_DIM
SSM_GROUPS = 4
HEADS_PER_GROUP = SSM_HEADS // SSM_GROUPS
D_STATE = 128
CONV_WIDTH = 5
CHUNK = 128
XBC_DIM = D_INNER + 2 * SSM_GROUPS * D_STATE
GROUP_DIM = D_INNER // SSM_GROUPS

LANES = 128
HEAD_SLOT = LANES
VMEM_LIMIT_BYTES = 56 * 1024 * 1024


def _params(sem, vmem=VMEM_LIMIT_BYTES):
    return pltpu.CompilerParams(dimension_semantics=sem, vmem_limit_bytes=vmem)


def _const_spec(shape):
    nd = len(shape)
    return pl.BlockSpec(shape, lambda *_: (0,) * nd, pipeline_mode=pl.Buffered(1))


def _rms(x, g):
    return x * lax.rsqrt(jnp.mean(x * x, axis=-1, keepdims=True) + EPS) * g


def _silu(x):
    return x * jax.nn.sigmoid(x)


def _dot(a, b):
    return jnp.dot(a, b, preferred_element_type=F32)


def _dot_split(a, b):
    hi = a.astype(BF16)
    lo = (a - hi.astype(F32)).astype(BF16)
    return _dot(hi, b) + _dot(lo, b)


def _ffn_kernel(x_ref, g_ref, wg_ref, wu_ref, wd_ref, o_ref):
    x = x_ref[0]
    hb = _rms(x, g_ref[...]).astype(BF16)
    gate = _dot(hb, wg_ref[...])
    up = _dot(hb, wu_ref[...])
    act = (_silu(gate) * up).astype(BF16)
    o_ref[0] = x + 0.5 * _dot(act, wd_ref[...])


def _ffn(x, g, wg, wu, wd, tm):
    b, s, d = x.shape
    tok = pl.BlockSpec((1, tm, d), lambda i, j: (i, j, 0))
    return pl.pallas_call(
        _ffn_kernel,
        out_shape=jax.ShapeDtypeStruct(x.shape, F32),
        grid=(b, s // tm),
        in_specs=[tok, _const_spec(g.shape), _const_spec(wg.shape), _const_spec(wu.shape),
                  _const_spec(wd.shape)],
        out_specs=tok,
        compiler_params=_params(("parallel", "parallel")),
        name="ffn",
    )(x, g, wg, wu, wd)


def _inproj_kernel(x_ref, g_ref, wcq, wckv, wkpe, wz, wxbc, wdt, wga, wgb,
                   ocq, ockv, okpe, oz, oxbc, odt, oga, ogb):
    hb = _rms(x_ref[0], g_ref[...]).astype(BF16)
    ocq[0] = _dot(hb, wcq[...])
    ockv[0] = _dot(hb, wckv[...])
    okpe[0] = _dot(hb, wkpe[...])
    oz[0] = _dot(hb, wz[...]).astype(BF16)
    oxbc[0] = _dot(hb, wxbc[...]).astype(BF16)
    odt[0] = _dot(hb, wdt[...])
    oga[0] = _dot(hb, wga[...]).astype(BF16)
    ogb[0] = _dot(hb, wgb[...]).astype(BF16)


def _inproj(x, g, ws, tm):
    b, s, d = x.shape
    dts = (F32, F32, F32, BF16, BF16, F32, BF16, BF16)
    tok = lambda n: pl.BlockSpec((1, tm, n), lambda i, j: (i, j, 0))
    return pl.pallas_call(
        _inproj_kernel,
        out_shape=[jax.ShapeDtypeStruct((b, s, w.shape[1]), dt) for w, dt in zip(ws, dts)],
        grid=(b, s // tm),
        in_specs=[tok(d), _const_spec(g.shape)] + [_const_spec(w.shape) for w in ws],
        out_specs=[tok(w.shape[1]) for w in ws],
        compiler_params=_params(("parallel", "parallel")),
        name="inproj",
    )(x, g, *ws)


def _mlaprep_kernel(cq_ref, ckv_ref, kpe_ref, pos_ref, qan, wq, kvan, wk, wv, qhn, khn, invf, sgn,
                    oq, ok, ov):
    qn = _rms(cq_ref[0], qan[...]).astype(BF16)
    kvn = _rms(ckv_ref[0], kvan[...]).astype(BF16)
    q_full = _dot(qn, wq[...])
    k_full = _dot(kvn, wk[...])
    v_full = _dot(kvn, wv[...])
    ones = jnp.ones((HEAD_SLOT - V_HEAD, v_full.shape[0]), BF16)
    for hp in range(N_HEADS_MLA // 2):
        vt = v_full[:, hp * LANES:(hp + 1) * LANES].T.astype(BF16)
        for i in range(2):
            base = (2 * hp + i) * HEAD_SLOT
            ov[0, 0, base:base + V_HEAD, :] = vt[i * V_HEAD:(i + 1) * V_HEAD, :]
            ov[0, 0, base + V_HEAD:base + HEAD_SLOT, :] = ones
    kpe = kpe_ref[0]
    ang = pos_ref[0].astype(F32) * invf[...]
    cos = jnp.cos(ang)
    sin = jnp.sin(ang) * sgn[...]
    lane = lax.broadcasted_iota(jnp.int32, cos.shape, 1)
    first_half = lane < QK_NOPE + QK_ROPE // 2

    def norm_rope(t, gain):
        t = t * lax.rsqrt(jnp.sum(t * t, axis=-1, keepdims=True) * (1.0 / QK_HEAD) + EPS) * gain
        partner = jnp.where(first_half,
                            pltpu.roll(t, LANES - QK_ROPE // 2, axis=1),
                            pltpu.roll(t, QK_ROPE // 2, axis=1))
        return t * cos + partner * sin

    for h in range(N_HEADS_MLA):
        sl = slice(h * HEAD_SLOT, (h + 1) * HEAD_SLOT)
        oq[0, :, sl] = norm_rope(q_full[:, sl], qhn[...]).astype(BF16)
        ok[0, :, sl] = norm_rope(k_full[:, sl] + kpe, khn[...]).astype(BF16)


def _mlaprep(cq, ckv, kpe, pos, consts, tm):
    b, s, _ = cq.shape
    tok = lambda n: pl.BlockSpec((1, tm, n), lambda i, j: (i, j, 0))
    hw = N_HEADS_MLA * HEAD_SLOT
    return pl.pallas_call(
        _mlaprep_kernel,
        out_shape=[jax.ShapeDtypeStruct((b, s, hw), BF16), jax.ShapeDtypeStruct((b, s, hw), BF16),
                   jax.ShapeDtypeStruct((b, s // tm, hw, tm), BF16)],
        grid=(b, s // tm),
        in_specs=[tok(Q_LORA), tok(KV_LORA), tok(LANES), tok(1)] + [_const_spec(c.shape) for c in consts],
        out_specs=[tok(hw), tok(hw), pl.BlockSpec((1, 1, hw, tm), lambda i, j: (i, j, 0, 0))],
        compiler_params=_params(("parallel", "parallel")),
        name="mlaprep",
    )(cq, ckv, kpe, pos, *consts)


def _flash_kernel(q_ref, k_ref, vt_ref, o_ref):
    tq = q_ref.shape[1]
    nk, _, tk = vt_ref.shape[1:]
    heads = (slice(0, HEAD_SLOT), slice(HEAD_SLOT, 2 * HEAD_SLOT))
    qs = [q_ref[0, :, sl] for sl in heads]

    def scores(j):
        return [lax.dot_general(k_ref[0, j * tk:(j + 1) * tk, sl], q, (((1,), (1,)), ((), ())),
                                preferred_element_type=F32) for sl, q in zip(heads, qs)]

    ms = [jnp.full((1, tq), -jnp.inf, F32)] * 2
    accs = [jnp.zeros((HEAD_SLOT, tq), F32)] * 2
    sts = scores(0)
    for j in range(nk):
        nxt = scores(j + 1) if j + 1 < nk else None
        for h, sl in enumerate(heads):
            m_new = jnp.maximum(ms[h], jnp.max(sts[h], axis=0, keepdims=True))
            alpha = jnp.exp2(ms[h] - m_new)
            pt = jnp.exp2(sts[h] - m_new).astype(BF16)
            accs[h] = alpha * accs[h] + _dot(vt_ref[0, j, sl, :], pt)
            ms[h] = m_new
        sts = nxt
    ot = jnp.concatenate([acc[0:V_HEAD] / acc[V_HEAD:V_HEAD + 1] for acc in accs], axis=0)
    o_ref[0] = ot.T.astype(o_ref.dtype)


def _flash(q, k, vt, tq):
    b, s, _ = q.shape
    nk, _, tk = vt.shape[1:]
    return pl.pallas_call(
        _flash_kernel,
        out_shape=jax.ShapeDtypeStruct((b, s, N_HEADS_MLA * V_HEAD), BF16),
        grid=(b, N_HEADS_MLA // 2, s // tq),
        in_specs=[pl.BlockSpec((1, tq, 2 * HEAD_SLOT), lambda i, p, j: (i, j, p)),
                  pl.BlockSpec((1, s, 2 * HEAD_SLOT), lambda i, p, j: (i, 0, p)),
                  pl.BlockSpec((1, nk, 2 * HEAD_SLOT, tk), lambda i, p, j: (i, 0, p, 0))],
        out_specs=pl.BlockSpec((1, tq, 2 * V_HEAD), lambda i, p, j: (i, j, p)),
        compiler_params=_params(("parallel", "parallel", "arbitrary")),
        name="flash",
    )(q, k, vt)


CONV_HALO = 8


def _conv_kernel(x_ref, w_ref, b_ref, o_ref, pad_ref, *, rows):
    s = x_ref.shape[1]
    zeros = jnp.zeros((CONV_HALO, pad_ref.shape[1]), F32)
    pad_ref[0:CONV_HALO, :] = zeros
    pad_ref[CONV_HALO + s:2 * CONV_HALO + s, :] = zeros
    for c in range(s // rows):
        pad_ref[CONV_HALO + c * rows:CONV_HALO + (c + 1) * rows, :] = (
            x_ref[0, c * rows:(c + 1) * rows, :].astype(F32))
    w = w_ref[...]
    for c in range(s // rows):
        acc = jnp.broadcast_to(b_ref[...], (rows, pad_ref.shape[1]))
        for t in range(CONV_WIDTH):
            lo = CONV_HALO + c * rows + t - CONV_WIDTH // 2
            acc = acc + w[t:t + 1, :] * pad_ref[lo:lo + rows, :]
        o_ref[0, c * rows:(c + 1) * rows, :] = _silu(acc).astype(o_ref.dtype)


def _conv(xbc, w, bias, rows):
    b, s, n = xbc.shape
    return pl.pallas_call(
        functools.partial(_conv_kernel, rows=rows),
        out_shape=jax.ShapeDtypeStruct(xbc.shape, BF16),
        grid=(b, n // GROUP_DIM),
        in_specs=[pl.BlockSpec((1, s, GROUP_DIM), lambda i, c: (i, 0, c)),
                  pl.BlockSpec((CONV_WIDTH, GROUP_DIM), lambda i, c: (0, c)),
                  pl.BlockSpec((1, GROUP_DIM), lambda i, c: (0, c))],
        out_specs=pl.BlockSpec((1, s, GROUP_DIM), lambda i, c: (i, 0, c)),
        scratch_shapes=[pltpu.VMEM((s + 2 * CONV_HALO, GROUP_DIM), F32)],
        compiler_params=_params(("parallel", "parallel")),
        name="conv",
    )(xbc, w, bias)


def _ssd_kernel(xs_ref, bm_ref, cm_ref, z_ref, dt_ref, bias_ref, alog_ref, dskip_ref, norm_ref,
                ef_ref, eb_ref, o_ref, y_scr, exp_scr, w_scr, sf_scr, sb_scr):
    L = CHUNK
    hpg = HEADS_PER_GROUP
    nc = xs_ref.shape[1] // L
    a = -jnp.exp(alog_ref[0])
    bias = bias_ref[0]
    ef = ef_ref[...]
    eb = eb_ref[...]
    li = lax.broadcasted_iota(jnp.int32, (L, L), 0)
    si = lax.broadcasted_iota(jnp.int32, (L, L), 1)
    tril = (si <= li).astype(BF16)
    triu = (si >= li).astype(BF16)
    lane = lax.broadcasted_iota(jnp.int32, (L, LANES), 1)
    is_fwd_lane = lane < hpg
    half = lane < SSM_HEAD_DIM

    sf_scr[...] = jnp.zeros_like(sf_scr)
    sb_scr[...] = jnp.zeros_like(sb_scr)

    def fwd_chunk(c, carry):
        rows = pl.ds(pl.multiple_of(c * L, L), L)
        dtv = jax.nn.softplus(dt_ref[0, rows, :] + bias)
        da = dtv * a
        cs = jnp.where(is_fwd_lane, _dot_split_left(tril, da), _dot_split_left(triu, da))
        tot = jnp.where(is_fwd_lane[0:1], cs[L - 1:L, :], cs[0:1, :])
        expc = jnp.exp(cs)
        wst = jnp.exp(tot - cs) * dtv
        exp_scr[rows, :] = expc
        w_scr[rows, :] = wst
        cs_t = cs.T
        dt_t = dtv.T
        xs = xs_ref[0, rows, :]
        bmat = bm_ref[0, rows, :]
        cmat = cm_ref[0, rows, :]
        cbm = lax.dot_general(cmat, bmat, (((1,), (1,)), ((), ())), preferred_element_type=F32)
        pairs = []
        for jp in range(hpg // 2):
            xp = xs[:, jp * LANES:(jp + 1) * LANES]
            ys = []
            for j in (2 * jp, 2 * jp + 1):
                arg = jnp.where(si <= li,
                                cs[:, j:j + 1] - cs_t[j:j + 1, :],
                                cs[:, hpg + j:hpg + j + 1] - cs_t[hpg + j:hpg + j + 1, :])
                dtf = dt_t[j:j + 1, :]
                dtb = dt_t[hpg + j:hpg + j + 1, :]
                mult = jnp.where(si < li, dtf, jnp.where(si > li, dtb, dtf + dtb))
                wmat = (cbm * jnp.exp(arg) * mult).astype(BF16)
                ys.append(_dot(wmat, xp))
            pairs.append(jnp.where(half, ys[0], ys[1]))
        y = jnp.concatenate(pairs, axis=1)
        y = y + _dot(cmat, sf_scr[...].astype(BF16)) * _dot_split(expc, ef)
        y_scr[rows, :] = y
        xw = (xs.astype(F32) * _dot_split(wst, ef)).astype(BF16)
        ds = lax.dot_general(bmat, xw, (((0,), (0,)), ((), ())), preferred_element_type=F32)
        sf_scr[...] = sf_scr[...] * _dot_split(expc[L - 1:L, :], ef) + ds
        return carry

    lax.fori_loop(0, nc, fwd_chunk, 0)

    dskip = dskip_ref[...]
    gnorm = norm_ref[...]

    def bwd_chunk(i, carry):
        c = nc - 1 - i
        rows = pl.ds(pl.multiple_of(c * L, L), L)
        expc = exp_scr[rows, :]
        wst = w_scr[rows, :]
        xs = xs_ref[0, rows, :]
        bmat = bm_ref[0, rows, :]
        cmat = cm_ref[0, rows, :]
        xf = xs.astype(F32)
        y = y_scr[rows, :] + _dot(cmat, sb_scr[...].astype(BF16)) * _dot_split(expc, eb)
        xw = (xf * _dot_split(wst, eb)).astype(BF16)
        ds = lax.dot_general(bmat, xw, (((0,), (0,)), ((), ())), preferred_element_type=F32)
        sb_scr[...] = sb_scr[...] * _dot_split(expc[0:1, :], eb) + ds
        y = (y + dskip * xf) * _silu(z_ref[0, rows, :].astype(F32))
        y = y * lax.rsqrt(jnp.mean(y * y, axis=-1, keepdims=True) + EPS) * gnorm
        o_ref[0, rows, :] = y.astype(o_ref.dtype)
        return carry

    lax.fori_loop(0, nc, bwd_chunk, 0)


def _dot_split_left(t, a):
    hi = a.astype(BF16)
    lo = (a - hi.astype(F32)).astype(BF16)
    return _dot(t, hi) + _dot(t, lo)


def _ssd(xc, z, dt, bias, alog, dskip, gnorm, ef, eb):
    b, s, _ = xc.shape
    nb = D_INNER // LANES
    grp = lambda n: pl.BlockSpec((1, s, n), lambda i, g: (i, 0, g))
    return pl.pallas_call(
        _ssd_kernel,
        out_shape=jax.ShapeDtypeStruct((b, s, D_INNER), BF16),
        grid=(b, SSM_GROUPS),
        in_specs=[grp(GROUP_DIM),
                  pl.BlockSpec((1, s, D_STATE), lambda i, g: (i, 0, nb + g)),
                  pl.BlockSpec((1, s, D_STATE), lambda i, g: (i, 0, nb + SSM_GROUPS + g)),
                  grp(GROUP_DIM), grp(LANES),
                  pl.BlockSpec((1, 1, LANES), lambda i, g: (g, 0, 0)),
                  pl.BlockSpec((1, 1, LANES), lambda i, g: (g, 0, 0)),
                  pl.BlockSpec((1, GROUP_DIM), lambda i, g: (0, g)),
                  pl.BlockSpec((1, GROUP_DIM), lambda i, g: (0, g)),
                  _const_spec(ef.shape), _const_spec(eb.shape)],
        out_specs=grp(GROUP_DIM),
        scratch_shapes=[pltpu.VMEM((s, GROUP_DIM), F32), pltpu.VMEM((s, LANES), F32),
                        pltpu.VMEM((s, LANES), F32), pltpu.VMEM((D_STATE, GROUP_DIM), F32),
                        pltpu.VMEM((D_STATE, GROUP_DIM), F32)],
        compiler_params=_params(("parallel", "parallel")),
        name="ssd",
    )(xc, xc, xc, z, dt, bias, alog, dskip, gnorm, ef, eb)


def _merge_kernel(x_ref, a_ref, m_ref, ga_ref, gb_ref, pa, pb, wo, o_ref):
    ba = _dot(a_ref[0], pa[...])
    bm = _dot(m_ref[0], pb[...])
    merged = (jax.nn.sigmoid(ga_ref[0].astype(F32)) * ba
              + jax.nn.sigmoid(gb_ref[0].astype(F32)) * bm)
    o_ref[0] = x_ref[0] + _dot(merged.astype(BF16), wo[...])


def _merge(x, a, m, ga, gb, pa, pb, wo, tm):
    b, s, d = x.shape
    tok = lambda n: pl.BlockSpec((1, tm, n), lambda i, j: (i, j, 0))
    return pl.pallas_call(
        _merge_kernel,
        out_shape=jax.ShapeDtypeStruct(x.shape, F32),
        grid=(b, s // tm),
        in_specs=[tok(d), tok(a.shape[2]), tok(m.shape[2]), tok(d), tok(d),
                  _const_spec(pa.shape), _const_spec(pb.shape), _const_spec(wo.shape)],
        out_specs=tok(d),
        compiler_params=_params(("parallel", "parallel")),
        name="merge",
    )(x, a, m, ga, gb, pa, pb, wo)


def _pad_heads(w, width):
    k = w.shape[0]
    w = w.reshape(k, N_HEADS_MLA, width)
    return jnp.pad(w, ((0, 0), (0, 0), (0, HEAD_SLOT - width))).reshape(k, N_HEADS_MLA * HEAD_SLOT)


def _group_lanes(v_f, v_b, fill):
    f = v_f.reshape(SSM_GROUPS, HEADS_PER_GROUP)
    bk = v_b.reshape(SSM_GROUPS, HEADS_PER_GROUP)
    rest = jnp.full((SSM_GROUPS, LANES - 2 * HEADS_PER_GROUP), fill, F32)
    return jnp.concatenate([f, bk, rest], axis=1).reshape(SSM_GROUPS, 1, LANES)


def _expanders():
    ch = np.arange(GROUP_DIM) // SSM_HEAD_DIM
    row = np.arange(LANES)[:, None]
    ef = (row == ch[None, :]).astype(np.float32)
    eb = (row == ch[None, :] + HEADS_PER_GROUP).astype(np.float32)
    return jnp.asarray(ef, BF16), jnp.asarray(eb, BF16)


def _rope_lanes():
    inv_freq = 1.0 / (ROPE_BASE ** (jnp.arange(0, QK_ROPE, 2, dtype=F32) / QK_ROPE))
    half = QK_ROPE // 2
    zeros = jnp.zeros((QK_NOPE,), F32)
    tail = jnp.zeros((HEAD_SLOT - QK_HEAD,), F32)
    invf = jnp.concatenate([zeros, inv_freq, inv_freq, tail]).reshape(1, HEAD_SLOT)
    sgn = jnp.concatenate([zeros, -jnp.ones((half,), F32), jnp.ones((half,), F32), tail])
    return invf, sgn.reshape(1, HEAD_SLOT)


def kernel(x, positions, ffn1_norm, ffn1_w_gate, ffn1_w_up, ffn1_w_down, mix_norm, w_in, q_a_norm, w_q_b, kv_a_norm, w_kv_b, q_head_norm, k_head_norm, conv_w, conv_b, a_log_fwd, a_log_bwd, dt_bias_fwd, dt_bias_bwd, d_skip, ssm_norm, w_attn_branch, w_ssm_branch, w_out, ffn2_norm, ffn2_w_gate, ffn2_w_up, ffn2_w_down):
    b, s, d = x.shape
    depth = ffn1_norm.shape[0]
    tm = min(512, s)
    bf = lambda w: w.astype(BF16)
    row = lambda v: v.reshape(1, -1).astype(F32)
    ef, eb = _expanders()
    invf, sgn = _rope_lanes()
    pos = positions.reshape(b, s, 1).astype(jnp.int32)

    for l in range(depth):
        x = _ffn(x, row(ffn1_norm[l]), bf(ffn1_w_gate[l]), bf(ffn1_w_up[l]), bf(ffn1_w_down[l]), tm)

        w = w_in[l]
        o = np.cumsum((0, Q_LORA, KV_LORA, QK_ROPE, D_INNER, XBC_DIM, SSM_HEADS, SSM_HEADS,
                       D_MODEL, D_MODEL))
        seg = [w[:, o[i]:o[i + 1]] for i in range(9)]
        w_kpe = jnp.pad(seg[2], ((0, 0), (QK_NOPE, HEAD_SLOT - QK_HEAD)))
        dtf = seg[5].reshape(d, SSM_GROUPS, HEADS_PER_GROUP)
        dtb = seg[6].reshape(d, SSM_GROUPS, HEADS_PER_GROUP)
        w_dt = jnp.pad(jnp.concatenate([dtf, dtb], axis=2),
                       ((0, 0), (0, 0), (0, LANES - 2 * HEADS_PER_GROUP))).reshape(d, SSM_GROUPS * LANES)
        ws = [bf(seg[0]), bf(seg[1]), bf(w_kpe), bf(seg[3]), bf(seg[4]), bf(w_dt), bf(seg[7]), bf(seg[8])]
        cq, ckv, kpe, z, xbc, dt, ga, gb = _inproj(x, row(mix_norm[l]), ws, tm)

        scale = math.log2(math.e) / math.sqrt(QK_HEAD)
        wkv = w_kv_b[l].reshape(KV_LORA, N_HEADS_MLA, QK_NOPE + V_HEAD)
        w_k = _pad_heads(wkv[:, :, :QK_NOPE].reshape(KV_LORA, N_HEADS_MLA * QK_NOPE), QK_NOPE)
        w_v = wkv[:, :, QK_NOPE:].reshape(KV_LORA, N_HEADS_MLA * V_HEAD)
        pad_gain = lambda g: jnp.pad(g, (0, HEAD_SLOT - QK_HEAD)).reshape(1, HEAD_SLOT)
        consts = [row(q_a_norm[l]), bf(_pad_heads(w_q_b[l], QK_HEAD)), row(kv_a_norm[l]), bf(w_k), bf(w_v),
                  pad_gain(q_head_norm[l] * scale), pad_gain(k_head_norm[l]), invf, sgn]
        q, k, vt = _mlaprep(cq, ckv, kpe, pos, consts, tm)
        a = _flash(q, k, vt, min(256, s))

        xc = _conv(xbc, conv_w[l].reshape(CONV_WIDTH, XBC_DIM), row(conv_b[l]), min(512, s))
        m = _ssd(xc, z, dt,
                 _group_lanes(dt_bias_fwd[l], dt_bias_bwd[l], 0.0),
                 _group_lanes(a_log_fwd[l], a_log_bwd[l], 0.0),
                 row(jnp.repeat(d_skip[l], SSM_HEAD_DIM)), row(ssm_norm[l]), ef, eb)

        x = _merge(x, a, m, ga, gb, bf(w_attn_branch[l]), bf(w_ssm_branch[l]), bf(w_out[l]), tm)
        x = _ffn(x, row(ffn2_norm[l]), bf(ffn2_w_gate[l]), bf(ffn2_w_up[l]), bf(ffn2_w_down[l]), tm)
    return x
```

```python
import functools
import math

import numpy as np
import jax
import jax.numpy as jnp
from jax import lax
from jax.experimental import pallas as pl
from jax.experimental.pallas import tpu as pltpu

F32 = jnp.float32
BF16 = jnp.bfloat16

D_MODEL = 1024
D_FF = 2816
EPS = 1e-6
N_HEADS_MLA = 16
QK_NOPE = 64
QK_ROPE = 32
QK_HEAD = QK_NOPE + QK_ROPE
V_HEAD = 64
Q_LORA = 384
KV_LORA = 256
ROPE_BASE = 10000.0
D_INNER = 2 * D_MODEL
SSM_HEAD_DIM = 64
SSM_HEADS = D_INNER // SSM_HEAD_DIM
SSM_GROUPS = 4
HEADS_PER_GROUP = SSM_HEADS // SSM_GROUPS
D_STATE = 128
CONV_WIDTH = 5
CHUNK = 128
XBC_DIM = D_INNER + 2 * SSM_GROUPS * D_STATE
GROUP_DIM = D_INNER // SSM_GROUPS

LANES = 128
HEAD_SLOT = LANES
VMEM_LIMIT_BYTES = 56 * 1024 * 1024


def _params(sem, vmem=VMEM_LIMIT_BYTES):
    return pltpu.CompilerParams(dimension_semantics=sem, vmem_limit_bytes=vmem)


def _const_spec(shape):
    nd = len(shape)
    return pl.BlockSpec(shape, lambda *_: (0,) * nd, pipeline_mode=pl.Buffered(1))


def _rms(x, g):
    return x * lax.rsqrt(jnp.mean(x * x, axis=-1, keepdims=True) + EPS) * g


def _silu(x):
    return x * jax.nn.sigmoid(x)


def _dot(a, b):
    return jnp.dot(a, b, preferred_element_type=F32)


def _dot_split(a, b):
    hi = a.astype(BF16)
    lo = (a - hi.astype(F32)).astype(BF16)
    return _dot(hi, b) + _dot(lo, b)


def _ffn_kernel(x_ref, g_ref, wg_ref, wu_ref, wd_ref, o_ref):
    x = x_ref[0]
    hb = _rms(x, g_ref[...]).astype(BF16)
    gate = _dot(hb, wg_ref[...])
    up = _dot(hb, wu_ref[...])
    act = (_silu(gate) * up).astype(BF16)
    o_ref[0] = x + 0.5 * _dot(act, wd_ref[...])


def _ffn(x, g, wg, wu, wd, tm):
    b, s, d = x.shape
    tok = pl.BlockSpec((1, tm, d), lambda i, j: (i, j, 0))
    return pl.pallas_call(
        _ffn_kernel,
        out_shape=jax.ShapeDtypeStruct(x.shape, F32),
        grid=(b, s // tm),
        in_specs=[tok, _const_spec(g.shape), _const_spec(wg.shape), _const_spec(wu.shape),
                  _const_spec(wd.shape)],
        out_specs=tok,
        compiler_params=_params(("parallel", "parallel")),
        name="ffn",
    )(x, g, wg, wu, wd)


def _inproj_kernel(x_ref, g_ref, wcq, wckv, wkpe, wz, wxbc, wdt, wga, wgb,
                   ocq, ockv, okpe, oz, oxbc, odt, oga, ogb):
    hb = _rms(x_ref[0], g_ref[...]).astype(BF16)
    ocq[0] = _dot(hb, wcq[...])
    ockv[0] = _dot(hb, wckv[...])
    okpe[0] = _dot(hb, wkpe[...])
    oz[0] = _dot(hb, wz[...]).astype(BF16)
    oxbc[0] = _dot(hb, wxbc[...]).astype(BF16)
    odt[0] = _dot(hb, wdt[...])
    oga[0] = _dot(hb, wga[...]).astype(BF16)
    ogb[0] = _dot(hb, wgb[...]).astype(BF16)


def _inproj(x, g, ws, tm):
    b, s, d = x.shape
    dts = (F32, F32, F32, BF16, BF16, F32, BF16, BF16)
    tok = lambda n: pl.BlockSpec((1, tm, n), lambda i, j: (i, j, 0))
    return pl.pallas_call(
        _inproj_kernel,
        out_shape=[jax.ShapeDtypeStruct((b, s, w.shape[1]), dt) for w, dt in zip(ws, dts)],
        grid=(b, s // tm),
        in_specs=[tok(d), _const_spec(g.shape)] + [_const_spec(w.shape) for w in ws],
        out_specs=[tok(w.shape[1]) for w in ws],
        compiler_params=_params(("parallel", "parallel")),
        name="inproj",
    )(x, g, *ws)


def _mlaprep_kernel(cq_ref, ckv_ref, kpe_ref, pos_ref, qan, wq, wq_sw, kvan, wk, wv, qhn, qhn_sw, khn, khn_sw,
                    invf, sgn, oq, ok, ov):
    qn = _rms(cq_ref[0], qan[...]).astype(BF16)
    kvn = _rms(ckv_ref[0], kvan[...]).astype(BF16)
    q_full = _dot(qn, wq[...])
    q_part = _dot(qn, wq_sw[...])
    k_full = _dot(kvn, wk[...])
    v_full = _dot(kvn, wv[...])
    ones = jnp.ones((HEAD_SLOT - V_HEAD, v_full.shape[0]), BF16)
    for hp in range(N_HEADS_MLA // 2):
        vt = v_full[:, hp * LANES:(hp + 1) * LANES].T.astype(BF16)
        for i in range(2):
            base = (2 * hp + i) * HEAD_SLOT
            ov[0, 0, base:base + V_HEAD, :] = vt[i * V_HEAD:(i + 1) * V_HEAD, :]
            ov[0, 0, base + V_HEAD:base + HEAD_SLOT, :] = ones
    kpe = kpe_ref[0]
    ang = pos_ref[0].astype(F32) * invf[...]
    cos = jnp.cos(ang)
    sin = jnp.sin(ang) * sgn[...]
    lane = lax.broadcasted_iota(jnp.int32, cos.shape, 1)
    kpe_part = jnp.where(lane < QK_NOPE + QK_ROPE // 2,
                         pltpu.roll(kpe, LANES - QK_ROPE // 2, axis=1),
                         pltpu.roll(kpe, QK_ROPE // 2, axis=1))
    q_cos, q_sin = qhn[...] * cos, qhn_sw[...] * sin
    k_cos, k_sin = khn[...] * cos, khn_sw[...] * sin
    k_rot = kpe_part * k_sin

    def inv_rms(t):
        return lax.rsqrt(jnp.sum(t * t, axis=-1, keepdims=True) * (1.0 / QK_HEAD) + EPS)

    for h in range(N_HEADS_MLA):
        sl = slice(h * HEAD_SLOT, (h + 1) * HEAD_SLOT)
        qh = q_full[:, sl]
        oq[0, :, sl] = ((qh * q_cos + q_part[:, sl] * q_sin) * inv_rms(qh)).astype(BF16)
        kh = k_full[:, sl] + kpe
        ok[0, :, sl] = ((kh * k_cos + k_rot) * inv_rms(kh)).astype(BF16)


def _mlaprep(cq, ckv, kpe, pos, consts, tm):
    b, s, _ = cq.shape
    tok = lambda n: pl.BlockSpec((1, tm, n), lambda i, j: (i, j, 0))
    hw = N_HEADS_MLA * HEAD_SLOT
    return pl.pallas_call(
        _mlaprep_kernel,
        out_shape=[jax.ShapeDtypeStruct((b, s, hw), BF16), jax.ShapeDtypeStruct((b, s, hw), BF16),
                   jax.ShapeDtypeStruct((b, s // tm, hw, tm), BF16)],
        grid=(b, s // tm),
        in_specs=[tok(Q_LORA), tok(KV_LORA), tok(LANES), tok(1)] + [_const_spec(c.shape) for c in consts],
        out_specs=[tok(hw), tok(hw), pl.BlockSpec((1, 1, hw, tm), lambda i, j: (i, j, 0, 0))],
        compiler_params=_params(("parallel", "parallel")),
        name="mlaprep",
    )(cq, ckv, kpe, pos, *consts)


def _flash_kernel(q_ref, k_ref, vt_ref, o_ref, *, tq, ts):
    nk, _, tk = vt_ref.shape[1:]
    ns = nk * tk // ts
    heads = (slice(0, HEAD_SLOT), slice(HEAD_SLOT, 2 * HEAD_SLOT))

    def q_tile(qi, carry):
        rows = pl.ds(pl.multiple_of(qi * tq, tq), tq)
        qs = [q_ref[0, rows, sl] for sl in heads]

        def scores(j):
            return [lax.dot_general(k_ref[0, j * ts:(j + 1) * ts, sl], q, (((1,), (1,)), ((), ())),
                                    preferred_element_type=F32) for sl, q in zip(heads, qs)]

        ms = [jnp.full((1, tq), -jnp.inf, F32)] * 2
        accs = [jnp.zeros((HEAD_SLOT, tq), F32)] * 2
        sts = scores(0)
        for j in range(ns):
            nxt = scores(j + 1) if j + 1 < ns else None
            tile, off = divmod(j * ts, tk)
            for h, sl in enumerate(heads):
                m_new = jnp.maximum(ms[h], jnp.max(sts[h], axis=0, keepdims=True))
                alpha = jnp.exp2(ms[h] - m_new)
                pt = jnp.exp2(sts[h] - m_new).astype(BF16)
                accs[h] = alpha * accs[h] + _dot(vt_ref[0, tile, sl, off:off + ts], pt)
                ms[h] = m_new
            sts = nxt
        ot = jnp.concatenate([acc[0:V_HEAD] / acc[V_HEAD:V_HEAD + 1] for acc in accs], axis=0)
        o_ref[0, rows, :] = ot.T.astype(o_ref.dtype)
        return carry

    lax.fori_loop(0, q_ref.shape[1] // tq, q_tile, 0)


def _flash(q, k, vt, tq, nq, ts):
    b, s, _ = q.shape
    nk, _, tk = vt.shape[1:]
    return pl.pallas_call(
        functools.partial(_flash_kernel, tq=tq, ts=ts),
        out_shape=jax.ShapeDtypeStruct((b, s, N_HEADS_MLA * V_HEAD), BF16),
        grid=(b, N_HEADS_MLA // 2, s // (tq * nq)),
        in_specs=[pl.BlockSpec((1, tq * nq, 2 * HEAD_SLOT), lambda i, p, j: (i, j, p)),
                  pl.BlockSpec((1, s, 2 * HEAD_SLOT), lambda i, p, j: (i, 0, p)),
                  pl.BlockSpec((1, nk, 2 * HEAD_SLOT, tk), lambda i, p, j: (i, 0, p, 0))],
        out_specs=pl.BlockSpec((1, tq * nq, 2 * V_HEAD), lambda i, p, j: (i, j, p)),
        compiler_params=_params(("parallel", "parallel", "arbitrary")),
        name="flash",
    )(q, k, vt)


CONV_HALO = 16
CONV_WINDOW = 256
CONV_ROWS = CONV_WINDOW - 2 * CONV_HALO


def _conv_kernel(x_ref, w_ref, b_ref, sh_ref, o_ref, pad_ref):
    s, n = x_ref.shape[1:]
    zeros = jnp.zeros((CONV_HALO, n), pad_ref.dtype)
    pad_ref[0:CONV_HALO, :] = zeros
    pad_ref[CONV_HALO + s:2 * CONV_HALO + s, :] = zeros
    pad_ref[CONV_HALO:CONV_HALO + s, :] = x_ref[0]
    w = w_ref[...]
    bias = jnp.broadcast_to(b_ref[...], (CONV_ROWS, n))
    for c in range(-(-s // CONV_ROWS)):
        r0 = min(c * CONV_ROWS, s - CONV_ROWS)
        win = pad_ref[r0:r0 + CONV_WINDOW, :]
        acc = bias
        for t in range(CONV_WIDTH):
            if t == CONV_WIDTH // 2:
                tap = win[CONV_HALO:CONV_HALO + CONV_ROWS, :].astype(F32)
            else:
                tap = _dot(sh_ref[t], win)
            acc = acc + w[t:t + 1, :] * tap
        o_ref[0, r0:r0 + CONV_ROWS, :] = _silu(acc).astype(o_ref.dtype)


def _conv_shifts():
    t = np.arange(CONV_WIDTH)[:, None, None]
    i = np.arange(CONV_ROWS)[None, :, None]
    j = np.arange(CONV_WINDOW)[None, None, :]
    return jnp.asarray(j == CONV_HALO + i + t - CONV_WIDTH // 2, BF16)


def _conv(xbc, w, bias):
    b, s, n = xbc.shape
    sh = _conv_shifts()
    return pl.pallas_call(
        _conv_kernel,
        out_shape=jax.ShapeDtypeStruct(xbc.shape, BF16),
        grid=(b, n // GROUP_DIM),
        in_specs=[pl.BlockSpec((1, s, GROUP_DIM), lambda i, c: (i, 0, c)),
                  pl.BlockSpec((CONV_WIDTH, GROUP_DIM), lambda i, c: (0, c)),
                  pl.BlockSpec((1, GROUP_DIM), lambda i, c: (0, c)),
                  _const_spec(sh.shape)],
        out_specs=pl.BlockSpec((1, s, GROUP_DIM), lambda i, c: (i, 0, c)),
        scratch_shapes=[pltpu.VMEM((s + 2 * CONV_HALO, GROUP_DIM), BF16)],
        compiler_params=_params(("parallel", "parallel")),
        name="conv",
    )(xbc, w, bias, sh)


def _ssd_kernel(xs_ref, bm_ref, cm_ref, z_ref, dt_ref, bias_ref, alog_ref, dskip_ref, norm_ref,
                ef_ref, eb_ref, o_ref, y_scr, cs_scr, exp_scr, w_scr, row_scr, sf_scr, sb_scr):
    L = CHUNK
    hpg = HEADS_PER_GROUP
    nrow = 3 * hpg
    nc = xs_ref.shape[1] // L
    a2 = -jnp.exp(alog_ref[0]) * math.log2(math.e)
    bias = bias_ref[0]
    ef = ef_ref[...]
    eb = eb_ref[...]
    li = lax.broadcasted_iota(jnp.int32, (L, L), 0)
    si = lax.broadcasted_iota(jnp.int32, (L, L), 1)
    tril = (si <= li).astype(BF16)
    triu = (si >= li).astype(BF16)
    below = si < li
    above = si > li
    lane = lax.broadcasted_iota(jnp.int32, (L, LANES), 1)
    is_fwd_lane = lane < hpg
    half = lane < SSM_HEAD_DIM

    def prologue(c, carry):
        rows = pl.ds(pl.multiple_of(c * L, L), L)
        dtv = jax.nn.softplus(dt_ref[0, rows, :] + bias)
        da = dtv * a2
        cs = jnp.where(is_fwd_lane, _dot_split_left(tril, da), _dot_split_left(triu, da))
        tot = jnp.where(is_fwd_lane[0:1], cs[L - 1:L, :], cs[0:1, :])
        cs_scr[rows, :] = cs
        exp_scr[rows, :] = jnp.exp2(cs)
        w_scr[rows, :] = jnp.exp2(tot - cs) * dtv
        both = jnp.log2(dtv + pltpu.roll(dtv, LANES - hpg, axis=1))
        packed = jnp.where(lane < 2 * hpg, cs - jnp.log2(dtv), pltpu.roll(both, 2 * hpg, axis=1))
        row_scr[pl.ds(pl.multiple_of(c * nrow, hpg), nrow), :] = packed.T[0:nrow, :]
        return carry

    lax.fori_loop(0, nc, prologue, 0)

    sf_scr[...] = jnp.zeros_like(sf_scr)
    sb_scr[...] = jnp.zeros_like(sb_scr)

    def fwd_chunk(c, carry):
        rows = pl.ds(pl.multiple_of(c * L, L), L)
        cs = cs_scr[rows, :]
        rowp = row_scr[pl.ds(pl.multiple_of(c * nrow, hpg), nrow), :]
        expc = exp_scr[rows, :]
        xs = xs_ref[0, rows, :]
        bmat = bm_ref[0, rows, :]
        cmat = cm_ref[0, rows, :]
        dec = _dot(expc.astype(BF16), ef)
        y_off = _dot(cmat, sf_scr[...].astype(BF16)) * dec
        xw = (xs.astype(F32) * _dot(w_scr[rows, :].astype(BF16), ef)).astype(BF16)
        ds = lax.dot_general(bmat, xw, (((0,), (0,)), ((), ())), preferred_element_type=F32)
        sf_scr[...] = sf_scr[...] * _dot_split(expc[L - 1:L, :], ef) + ds
        cbm = lax.dot_general(cmat, bmat, (((1,), (1,)), ((), ())), preferred_element_type=F32)
        pairs = []
        for jp in range(hpg // 2):
            ws = []
            for j in (2 * jp, 2 * jp + 1):
                arg = jnp.where(below, cs[:, j:j + 1] - rowp[j:j + 1, :],
                                jnp.where(above, cs[:, hpg + j:hpg + j + 1] - rowp[hpg + j:hpg + j + 1, :],
                                          rowp[2 * hpg + j:2 * hpg + j + 1, :]))
                ws.append((cbm * jnp.exp2(arg)).astype(BF16))
            xp = xs[:, jp * LANES:(jp + 1) * LANES]
            zero = jnp.zeros_like(xp)
            rhs = jnp.concatenate([jnp.where(half, xp, zero), jnp.where(half, zero, xp)], axis=0)
            pairs.append(_dot(jnp.concatenate(ws, axis=1), rhs))
        y_scr[rows, :] = jnp.concatenate(pairs, axis=1) + y_off
        return carry

    lax.fori_loop(0, nc, fwd_chunk, 0)

    dskip = dskip_ref[...]
    gnorm = norm_ref[...]

    def bwd_chunk(i, carry):
        c = nc - 1 - i
        rows = pl.ds(pl.multiple_of(c * L, L), L)
        expc = exp_scr[rows, :]
        wst = w_scr[rows, :]
        xs = xs_ref[0, rows, :]
        bmat = bm_ref[0, rows, :]
        cmat = cm_ref[0, rows, :]
        xf = xs.astype(F32)
        y = y_scr[rows, :] + _dot(cmat, sb_scr[...].astype(BF16)) * _dot(expc.astype(BF16), eb)
        xw = (xf * _dot(wst.astype(BF16), eb)).astype(BF16)
        ds = lax.dot_general(bmat, xw, (((0,), (0,)), ((), ())), preferred_element_type=F32)
        sb_scr[...] = sb_scr[...] * _dot_split(expc[0:1, :], eb) + ds
        y = (y + dskip * xf) * _silu(z_ref[0, rows, :].astype(F32))
        y = y * lax.rsqrt(jnp.mean(y * y, axis=-1, keepdims=True) + EPS) * gnorm
        o_ref[0, rows, :] = y.astype(o_ref.dtype)
        return carry

    lax.fori_loop(0, nc, bwd_chunk, 0)


def _dot_split_left(t, a):
    hi = a.astype(BF16)
    lo = (a - hi.astype(F32)).astype(BF16)
    return _dot(t, hi) + _dot(t, lo)


def _ssd(xc, z, dt, bias, alog, dskip, gnorm, ef, eb):
    b, s, _ = xc.shape
    nb = D_INNER // LANES
    grp = lambda n: pl.BlockSpec((1, s, n), lambda i, g: (i, 0, g))
    return pl.pallas_call(
        _ssd_kernel,
        out_shape=jax.ShapeDtypeStruct((b, s, D_INNER), BF16),
        grid=(b, SSM_GROUPS),
        in_specs=[grp(GROUP_DIM),
                  pl.BlockSpec((1, s, D_STATE), lambda i, g: (i, 0, nb + g)),
                  pl.BlockSpec((1, s, D_STATE), lambda i, g: (i, 0, nb + SSM_GROUPS + g)),
                  grp(GROUP_DIM), grp(LANES),
                  pl.BlockSpec((1, 1, LANES), lambda i, g: (g, 0, 0)),
                  pl.BlockSpec((1, 1, LANES), lambda i, g: (g, 0, 0)),
                  pl.BlockSpec((1, GROUP_DIM), lambda i, g: (0, g)),
                  pl.BlockSpec((1, GROUP_DIM), lambda i, g: (0, g)),
                  _const_spec(ef.shape), _const_spec(eb.shape)],
        out_specs=grp(GROUP_DIM),
        scratch_shapes=[pltpu.VMEM((s, GROUP_DIM), F32), pltpu.VMEM((s, LANES), F32),
                        pltpu.VMEM((s, LANES), F32), pltpu.VMEM((s, LANES), F32),
                        pltpu.VMEM((s // CHUNK * 3 * HEADS_PER_GROUP, CHUNK), F32),
                        pltpu.VMEM((D_STATE, GROUP_DIM), F32), pltpu.VMEM((D_STATE, GROUP_DIM), F32)],
        compiler_params=_params(("parallel", "parallel")),
        name="ssd",
    )(xc, xc, xc, z, dt, bias, alog, dskip, gnorm, ef, eb)


def _merge_kernel(x_ref, a_ref, m_ref, ga_ref, gb_ref, pa, pb, wo, o_ref):
    ba = _dot(a_ref[0], pa[...])
    bm = _dot(m_ref[0], pb[...])
    merged = (jax.nn.sigmoid(ga_ref[0].astype(F32)) * ba
              + jax.nn.sigmoid(gb_ref[0].astype(F32)) * bm)
    o_ref[0] = x_ref[0] + _dot(merged.astype(BF16), wo[...])


def _merge(x, a, m, ga, gb, pa, pb, wo, tm):
    b, s, d = x.shape
    tok = lambda n: pl.BlockSpec((1, tm, n), lambda i, j: (i, j, 0))
    return pl.pallas_call(
        _merge_kernel,
        out_shape=jax.ShapeDtypeStruct(x.shape, F32),
        grid=(b, s // tm),
        in_specs=[tok(d), tok(a.shape[2]), tok(m.shape[2]), tok(d), tok(d),
                  _const_spec(pa.shape), _const_spec(pb.shape), _const_spec(wo.shape)],
        out_specs=tok(d),
        compiler_params=_params(("parallel", "parallel")),
        name="merge",
    )(x, a, m, ga, gb, pa, pb, wo)


def _pad_heads(w, width):
    k = w.shape[0]
    w = w.reshape(k, N_HEADS_MLA, width)
    return jnp.pad(w, ((0, 0), (0, 0), (0, HEAD_SLOT - width))).reshape(k, N_HEADS_MLA * HEAD_SLOT)


def _group_lanes(v_f, v_b, fill):
    f = v_f.reshape(SSM_GROUPS, HEADS_PER_GROUP)
    bk = v_b.reshape(SSM_GROUPS, HEADS_PER_GROUP)
    rest = jnp.full((SSM_GROUPS, LANES - 2 * HEADS_PER_GROUP), fill, F32)
    return jnp.concatenate([f, bk, rest], axis=1).reshape(SSM_GROUPS, 1, LANES)


def _expanders():
    ch = np.arange(GROUP_DIM) // SSM_HEAD_DIM
    row = np.arange(LANES)[:, None]
    ef = (row == ch[None, :]).astype(np.float32)
    eb = (row == ch[None, :] + HEADS_PER_GROUP).astype(np.float32)
    return jnp.asarray(ef, BF16), jnp.asarray(eb, BF16)


_ROPE_SWAP = np.concatenate([np.arange(QK_NOPE), np.arange(QK_NOPE + QK_ROPE // 2, QK_HEAD),
                             np.arange(QK_NOPE, QK_NOPE + QK_ROPE // 2), np.arange(QK_HEAD, HEAD_SLOT)])


def _rope_lanes():
    inv_freq = 1.0 / (ROPE_BASE ** (jnp.arange(0, QK_ROPE, 2, dtype=F32) / QK_ROPE))
    half = QK_ROPE // 2
    zeros = jnp.zeros((QK_NOPE,), F32)
    tail = jnp.zeros((HEAD_SLOT - QK_HEAD,), F32)
    invf = jnp.concatenate([zeros, inv_freq, inv_freq, tail]).reshape(1, HEAD_SLOT)
    sgn = jnp.concatenate([zeros, -jnp.ones((half,), F32), jnp.ones((half,), F32), tail])
    return invf, sgn.reshape(1, HEAD_SLOT)


def kernel(x, positions, ffn1_norm, ffn1_w_gate, ffn1_w_up, ffn1_w_down, mix_norm, w_in, q_a_norm, w_q_b, kv_a_norm, w_kv_b, q_head_norm, k_head_norm, conv_w, conv_b, a_log_fwd, a_log_bwd, dt_bias_fwd, dt_bias_bwd, d_skip, ssm_norm, w_attn_branch, w_ssm_branch, w_out, ffn2_norm, ffn2_w_gate, ffn2_w_up, ffn2_w_down):
    b, s, d = x.shape
    depth = ffn1_norm.shape[0]
    tm = min(512, s)
    bf = lambda w: w.astype(BF16)
    row = lambda v: v.reshape(1, -1).astype(F32)
    ef, eb = _expanders()
    invf, sgn = _rope_lanes()
    pos = positions.reshape(b, s, 1).astype(jnp.int32)

    for l in range(depth):
        x = _ffn(x, row(ffn1_norm[l]), bf(ffn1_w_gate[l]), bf(ffn1_w_up[l]), bf(ffn1_w_down[l]), tm)

        w = w_in[l]
        o = np.cumsum((0, Q_LORA, KV_LORA, QK_ROPE, D_INNER, XBC_DIM, SSM_HEADS, SSM_HEADS,
                       D_MODEL, D_MODEL))
        seg = [w[:, o[i]:o[i + 1]] for i in range(9)]
        w_kpe = jnp.pad(seg[2], ((0, 0), (QK_NOPE, HEAD_SLOT - QK_HEAD)))
        dtf = seg[5].reshape(d, SSM_GROUPS, HEADS_PER_GROUP)
        dtb = seg[6].reshape(d, SSM_GROUPS, HEADS_PER_GROUP)
        w_dt = jnp.pad(jnp.concatenate([dtf, dtb], axis=2),
                       ((0, 0), (0, 0), (0, LANES - 2 * HEADS_PER_GROUP))).reshape(d, SSM_GROUPS * LANES)
        ws = [bf(seg[0]), bf(seg[1]), bf(w_kpe), bf(seg[3]), bf(seg[4]), bf(w_dt), bf(seg[7]), bf(seg[8])]
        cq, ckv, kpe, z, xbc, dt, ga, gb = _inproj(x, row(mix_norm[l]), ws, tm)

        scale = math.log2(math.e) / math.sqrt(QK_HEAD)
        wkv = w_kv_b[l].reshape(KV_LORA, N_HEADS_MLA, QK_NOPE + V_HEAD)
        w_k = _pad_heads(wkv[:, :, :QK_NOPE].reshape(KV_LORA, N_HEADS_MLA * QK_NOPE), QK_NOPE)
        w_v = wkv[:, :, QK_NOPE:].reshape(KV_LORA, N_HEADS_MLA * V_HEAD)
        pad_gain = lambda g: jnp.pad(g, (0, HEAD_SLOT - QK_HEAD)).reshape(1, HEAD_SLOT)
        w_q = _pad_heads(w_q_b[l], QK_HEAD)
        w_q_sw = w_q.reshape(Q_LORA, N_HEADS_MLA, HEAD_SLOT)[:, :, _ROPE_SWAP].reshape(w_q.shape)
        g_q, g_k = pad_gain(q_head_norm[l] * scale), pad_gain(k_head_norm[l])
        consts = [row(q_a_norm[l]), bf(w_q), bf(w_q_sw), row(kv_a_norm[l]), bf(w_k), bf(w_v),
                  g_q, g_q[:, _ROPE_SWAP], g_k, g_k[:, _ROPE_SWAP], invf, sgn]
        q, k, vt = _mlaprep(cq, ckv, kpe, pos, consts, tm)
        a = _flash(q, k, vt, tq=min(256, s), nq=2 if s >= 512 else 1, ts=tm)

        xc = _conv(xbc, conv_w[l].reshape(CONV_WIDTH, XBC_DIM), row(conv_b[l]))
        m = _ssd(xc, z, dt,
                 _group_lanes(dt_bias_fwd[l], dt_bias_bwd[l], 0.0),
                 _group_lanes(a_log_fwd[l], a_log_bwd[l], 0.0),
                 row(jnp.repeat(d_skip[l], SSM_HEAD_DIM)), row(ssm_norm[l]), ef, eb)

        x = _merge(x, a, m, ga, gb, bf(w_attn_branch[l]), bf(w_ssm_branch[l]), bf(w_out[l]), tm)
        x = _ffn(x, row(ffn2_norm[l]), bf(ffn2_w_gate[l]), bf(ffn2_w_up[l]), bf(ffn2_w_down[l]), tm)
    return x
```

```python
import functools
import math

import numpy as np
import jax
import jax.numpy as jnp
from jax import lax
from jax.experimental import pallas as pl
from jax.experimental.pallas import tpu as pltpu

F32 = jnp.float32
BF16 = jnp.bfloat16

D_MODEL = 1024
D_FF = 2816
EPS = 1e-6
N_HEADS_MLA = 16
QK_NOPE = 64
QK_ROPE = 32
QK_HEAD = QK_NOPE + QK_ROPE
V_HEAD = 64
Q_LORA = 384
KV_LORA = 256
ROPE_BASE = 10000.0
D_INNER = 2 * D_MODEL
SSM_HEAD_DIM = 64
SSM_HEADS = D_INNER // SSM_HEAD_DIM
SSM_GROUPS = 4
HEADS_PER_GROUP = SSM_HEADS // SSM_GROUPS
D_STATE = 128
CONV_WIDTH = 5
CHUNK = 128
XBC_DIM = D_INNER + 2 * SSM_GROUPS * D_STATE
GROUP_DIM = D_INNER // SSM_GROUPS

LANES = 128
HEAD_SLOT = LANES
VMEM_LIMIT_BYTES = 56 * 1024 * 1024


def _params(sem, vmem=VMEM_LIMIT_BYTES):
    return pltpu.CompilerParams(dimension_semantics=sem, vmem_limit_bytes=vmem)


def _const_spec(shape):
    nd = len(shape)
    return pl.BlockSpec(shape, lambda *_: (0,) * nd, pipeline_mode=pl.Buffered(1))


def _rms(x, g):
    return x * lax.rsqrt(jnp.mean(x * x, axis=-1, keepdims=True) + EPS) * g


def _silu(x):
    return x * jax.nn.sigmoid(x)


def _dot(a, b):
    return jnp.dot(a, b, preferred_element_type=F32)


def _split3(a):
    hi = a.astype(BF16).astype(F32)
    mid = (a - hi).astype(BF16).astype(F32)
    lo = (a - hi - mid).astype(BF16).astype(F32)
    return hi, mid, lo


def _dot_split(a, b):
    hi = a.astype(BF16)
    lo = (a - hi.astype(F32)).astype(BF16)
    return _dot(hi, b) + _dot(lo, b)


def _ffn_kernel(x_ref, g_ref, wg_ref, wu_ref, wd_ref, o_ref):
    x = x_ref[0]
    hb = _rms(x, g_ref[...]).astype(BF16)
    gate = _dot(hb, wg_ref[...])
    up = _dot(hb, wu_ref[...])
    act = (_silu(gate) * up).astype(BF16)
    o_ref[0] = x + 0.5 * _dot(act, wd_ref[...])


def _ffn(x, g, wg, wu, wd, tm):
    b, s, d = x.shape
    tok = pl.BlockSpec((1, tm, d), lambda i, j: (i, j, 0))
    return pl.pallas_call(
        _ffn_kernel,
        out_shape=jax.ShapeDtypeStruct(x.shape, F32),
        grid=(b, s // tm),
        in_specs=[tok, _const_spec(g.shape), _const_spec(wg.shape), _const_spec(wu.shape),
                  _const_spec(wd.shape)],
        out_specs=tok,
        compiler_params=_params(("parallel", "parallel")),
        name="ffn",
    )(x, g, wg, wu, wd)


def _inproj_kernel(x_ref, g_ref, wcq, wckv, wkpe, wz, wxbc, wdt, wga, wgb,
                   ocq, ockv, okpe, oz, oxbc, odt, oga, ogb):
    hb = _rms(x_ref[0], g_ref[...]).astype(BF16)
    ocq[0] = _dot(hb, wcq[...])
    ockv[0] = _dot(hb, wckv[...])
    okpe[0] = _dot(hb, wkpe[...])
    oz[0] = _dot(hb, wz[...]).astype(BF16)
    oxbc[0] = _dot(hb, wxbc[...]).astype(BF16)
    odt[0] = _dot(hb, wdt[...])
    oga[0] = _dot(hb, wga[...]).astype(BF16)
    ogb[0] = _dot(hb, wgb[...]).astype(BF16)


def _inproj(x, g, ws, tm):
    b, s, d = x.shape
    dts = (F32, F32, F32, BF16, BF16, F32, BF16, BF16)
    tok = lambda n: pl.BlockSpec((1, tm, n), lambda i, j: (i, j, 0))
    return pl.pallas_call(
        _inproj_kernel,
        out_shape=[jax.ShapeDtypeStruct((b, s, w.shape[1]), dt) for w, dt in zip(ws, dts)],
        grid=(b, s // tm),
        in_specs=[tok(d), _const_spec(g.shape)] + [_const_spec(w.shape) for w in ws],
        out_specs=[tok(w.shape[1]) for w in ws],
        compiler_params=_params(("parallel", "parallel")),
        name="inproj",
    )(x, g, *ws)


def _mlaprep_kernel(cq_ref, ckv_ref, kpe_ref, pos_ref, qan, wq, wq_sw, kvan, wk, wv, qhn, qhn_sw, khn, khn_sw,
                    invf, sgn, oq, ok, ov):
    qn = _rms(cq_ref[0], qan[...]).astype(BF16)
    kvn = _rms(ckv_ref[0], kvan[...]).astype(BF16)
    q_full = _dot(qn, wq[...])
    q_part = _dot(qn, wq_sw[...])
    k_full = _dot(kvn, wk[...])
    v_full = _dot(kvn, wv[...])
    ones = jnp.ones((HEAD_SLOT - V_HEAD, v_full.shape[0]), BF16)
    for hp in range(N_HEADS_MLA // 2):
        vt = v_full[:, hp * LANES:(hp + 1) * LANES].T.astype(BF16)
        for i in range(2):
            base = (2 * hp + i) * HEAD_SLOT
            ov[0, 0, base:base + V_HEAD, :] = vt[i * V_HEAD:(i + 1) * V_HEAD, :]
            ov[0, 0, base + V_HEAD:base + HEAD_SLOT, :] = ones
    kpe = kpe_ref[0]
    ang = pos_ref[0].astype(F32) * invf[...]
    cos = jnp.cos(ang)
    sin = jnp.sin(ang) * sgn[...]
    lane = lax.broadcasted_iota(jnp.int32, cos.shape, 1)
    kpe_part = jnp.where(lane < QK_NOPE + QK_ROPE // 2,
                         pltpu.roll(kpe, LANES - QK_ROPE // 2, axis=1),
                         pltpu.roll(kpe, QK_ROPE // 2, axis=1))
    q_cos, q_sin = qhn[...] * cos, qhn_sw[...] * sin
    k_cos, k_sin = khn[...] * cos, khn_sw[...] * sin
    k_rot = kpe_part * k_sin

    def inv_rms(t):
        return lax.rsqrt(jnp.sum(t * t, axis=-1, keepdims=True) * (1.0 / QK_HEAD) + EPS)

    for h in range(N_HEADS_MLA):
        sl = slice(h * HEAD_SLOT, (h + 1) * HEAD_SLOT)
        qh = q_full[:, sl]
        oq[0, :, sl] = ((qh * q_cos + q_part[:, sl] * q_sin) * inv_rms(qh)).astype(BF16)
        kh = k_full[:, sl] + kpe
        ok[0, :, sl] = ((kh * k_cos + k_rot) * inv_rms(kh)).astype(BF16)


def _mlaprep(cq, ckv, kpe, pos, consts, tm):
    b, s, _ = cq.shape
    tok = lambda n: pl.BlockSpec((1, tm, n), lambda i, j: (i, j, 0))
    hw = N_HEADS_MLA * HEAD_SLOT
    return pl.pallas_call(
        _mlaprep_kernel,
        out_shape=[jax.ShapeDtypeStruct((b, s, hw), BF16), jax.ShapeDtypeStruct((b, s, hw), BF16),
                   jax.ShapeDtypeStruct((b, s // tm, hw, tm), BF16)],
        grid=(b, s // tm),
        in_specs=[tok(Q_LORA), tok(KV_LORA), tok(LANES), tok(1)] + [_const_spec(c.shape) for c in consts],
        out_specs=[tok(hw), tok(hw), pl.BlockSpec((1, 1, hw, tm), lambda i, j: (i, j, 0, 0))],
        compiler_params=_params(("parallel", "parallel")),
        name="mlaprep",
    )(cq, ckv, kpe, pos, *consts)


def _flash_kernel(q_ref, k_ref, vt_ref, o_ref, *, tq, ts):
    nk, _, tk = vt_ref.shape[1:]
    ns = nk * tk // ts
    heads = (slice(0, HEAD_SLOT), slice(HEAD_SLOT, 2 * HEAD_SLOT))

    def q_tile(qi, carry):
        rows = pl.ds(pl.multiple_of(qi * tq, tq), tq)
        qs = [q_ref[0, rows, sl] for sl in heads]

        def scores(j):
            return [lax.dot_general(k_ref[0, j * ts:(j + 1) * ts, sl], q, (((1,), (1,)), ((), ())),
                                    preferred_element_type=F32) for sl, q in zip(heads, qs)]

        ms = [jnp.full((1, tq), -jnp.inf, F32)] * 2
        accs = [jnp.zeros((HEAD_SLOT, tq), F32)] * 2
        ahead = 3
        queue = [scores(j) for j in range(min(ahead, ns))]
        for j in range(ns):
            if j + ahead < ns:
                queue.append(scores(j + ahead))
            sts = queue.pop(0)
            tile, off = divmod(j * ts, tk)
            for h, sl in enumerate(heads):
                m_new = jnp.maximum(ms[h], jnp.max(sts[h], axis=0, keepdims=True))
                alpha = jnp.exp2(ms[h] - m_new)
                pt = jnp.exp2(sts[h] - m_new).astype(BF16)
                accs[h] = alpha * accs[h] + _dot(vt_ref[0, tile, sl, off:off + ts], pt)
                ms[h] = m_new
        ot = jnp.concatenate([acc[0:V_HEAD] / acc[V_HEAD:V_HEAD + 1] for acc in accs], axis=0)
        o_ref[0, rows, :] = ot.T.astype(o_ref.dtype)
        return carry

    lax.fori_loop(0, q_ref.shape[1] // tq, q_tile, 0)


def _flash(q, k, vt, tq, nq, ts):
    b, s, _ = q.shape
    nk, _, tk = vt.shape[1:]
    return pl.pallas_call(
        functools.partial(_flash_kernel, tq=tq, ts=ts),
        out_shape=jax.ShapeDtypeStruct((b, s, N_HEADS_MLA * V_HEAD), BF16),
        grid=(b, N_HEADS_MLA // 2, s // (tq * nq)),
        in_specs=[pl.BlockSpec((1, tq * nq, 2 * HEAD_SLOT), lambda i, p, j: (i, j, p)),
                  pl.BlockSpec((1, s, 2 * HEAD_SLOT), lambda i, p, j: (i, 0, p)),
                  pl.BlockSpec((1, nk, 2 * HEAD_SLOT, tk), lambda i, p, j: (i, 0, p, 0))],
        out_specs=pl.BlockSpec((1, tq * nq, 2 * V_HEAD), lambda i, p, j: (i, j, p)),
        compiler_params=_params(("parallel", "parallel", "arbitrary")),
        name="flash",
    )(q, k, vt)


CONV_HALO = 16
CONV_WINDOW = 256
CONV_ROWS = CONV_WINDOW - 2 * CONV_HALO


def _conv_kernel(x_ref, w_ref, b_ref, sh_ref, o_ref, pad_ref):
    s, n = x_ref.shape[1:]
    zeros = jnp.zeros((CONV_HALO, n), pad_ref.dtype)
    pad_ref[0:CONV_HALO, :] = zeros
    pad_ref[CONV_HALO + s:2 * CONV_HALO + s, :] = zeros
    pad_ref[CONV_HALO:CONV_HALO + s, :] = x_ref[0]
    w = w_ref[...]
    bias = jnp.broadcast_to(b_ref[...], (CONV_ROWS, n))
    for c in range(-(-s // CONV_ROWS)):
        r0 = min(c * CONV_ROWS, s - CONV_ROWS)
        win = pad_ref[r0:r0 + CONV_WINDOW, :]
        acc = bias
        for t in range(CONV_WIDTH):
            if t == CONV_WIDTH // 2:
                tap = win[CONV_HALO:CONV_HALO + CONV_ROWS, :].astype(F32)
            else:
                tap = _dot(sh_ref[t], win)
            acc = acc + w[t:t + 1, :] * tap
        o_ref[0, r0:r0 + CONV_ROWS, :] = _silu(acc).astype(o_ref.dtype)


def _conv_shifts():
    t = np.arange(CONV_WIDTH)[:, None, None]
    i = np.arange(CONV_ROWS)[None, :, None]
    j = np.arange(CONV_WINDOW)[None, None, :]
    return jnp.asarray(j == CONV_HALO + i + t - CONV_WIDTH // 2, BF16)


def _conv(xbc, w, bias):
    b, s, n = xbc.shape
    sh = _conv_shifts()
    return pl.pallas_call(
        _conv_kernel,
        out_shape=jax.ShapeDtypeStruct(xbc.shape, BF16),
        grid=(b, n // GROUP_DIM),
        in_specs=[pl.BlockSpec((1, s, GROUP_DIM), lambda i, c: (i, 0, c)),
                  pl.BlockSpec((CONV_WIDTH, GROUP_DIM), lambda i, c: (0, c)),
                  pl.BlockSpec((1, GROUP_DIM), lambda i, c: (0, c)),
                  _const_spec(sh.shape)],
        out_specs=pl.BlockSpec((1, s, GROUP_DIM), lambda i, c: (i, 0, c)),
        scratch_shapes=[pltpu.VMEM((s + 2 * CONV_HALO, GROUP_DIM), BF16)],
        compiler_params=_params(("parallel", "parallel")),
        name="conv",
    )(xbc, w, bias, sh)


def _ssd_kernel(xs_ref, bm_ref, cm_ref, z_ref, dt_ref, bias_ref, alog_ref, dskip_ref, norm_ref,
                ef_ref, eb_ref, o_ref, y_scr, a_scr, exp_scr, w_scr, row_scr, rhs_scr, sf_scr, sb_scr):
    L = CHUNK
    hpg = HEADS_PER_GROUP
    nl = 2 * hpg
    nc = xs_ref.shape[1] // L
    a2 = -jnp.exp(alog_ref[0]) * math.log2(math.e)
    bias = bias_ref[0]
    ef = ef_ref[...]
    eb = eb_ref[...]
    li = lax.broadcasted_iota(jnp.int32, (L, L), 0)
    si = lax.broadcasted_iota(jnp.int32, (L, L), 1)
    tril = (si <= li).astype(BF16)
    triu = (si >= li).astype(BF16)
    below = si < li
    above = si > li
    lane = lax.broadcasted_iota(jnp.int32, (L, LANES), 1)
    is_fwd_lane = lane < hpg
    half = lane < SSM_HEAD_DIM
    fwd_rows = lax.broadcasted_iota(jnp.int32, (nl, L), 0) < hpg
    ri = lax.broadcasted_iota(jnp.int32, (3 * nl, 2 * L), 0)
    ci = lax.broadcasted_iota(jnp.int32, (3 * nl, 2 * L), 1)
    ones_rows = ((ri % nl < hpg) == (ci < L)).astype(F32)
    zero_rows = jnp.zeros((nl, 2 * L), F32)
    head_lanes = [lane % hpg == j for j in range(hpg)]

    def prologue(c, carry):
        rows = pl.ds(pl.multiple_of(c * L, L), L)
        dtv = jax.nn.softplus(dt_ref[0, rows, :] + bias)
        da = dtv * a2
        cs = jnp.where(is_fwd_lane, _dot_split_left(tril, da), _dot_split_left(triu, da))
        tot = jnp.where(is_fwd_lane[0:1], cs[L - 1:L, :], cs[0:1, :])
        exp_scr[rows, :] = jnp.exp2(cs)
        w_scr[rows, :] = jnp.exp2(tot - cs) * dtv
        q_hi, q_mid, q_lo = _split3(cs)
        a_scr[rows, :] = jnp.where(
            lane < nl, q_hi, jnp.where(
                lane < 2 * nl, pltpu.roll(q_mid, nl, axis=1), jnp.where(
                    lane < 3 * nl, pltpu.roll(q_lo, 2 * nl, axis=1), jnp.where(
                        (lane >= LANES // 2) & (lane < LANES // 2 + 3 * nl), 1.0, 0.0)))).astype(BF16)
        both = jnp.log2(dtv + pltpu.roll(dtv, LANES - hpg, axis=1))
        packed = jnp.where(lane < nl, cs - jnp.log2(dtv), pltpu.roll(both, nl, axis=1))
        pt = packed.T
        row_scr[pl.ds(pl.multiple_of(c * hpg, hpg), hpg), :] = pt[nl:nl + hpg, :]
        pieces = []
        for piece in _split3(pt[0:nl, :]):
            pieces.append(jnp.concatenate([jnp.where(fwd_rows, -piece, 0.0),
                                           jnp.where(fwd_rows, 0.0, -piece)], axis=1))
        rhs_scr[pl.ds(pl.multiple_of(c * LANES, LANES), LANES), :] = jnp.concatenate(
            [ones_rows, zero_rows] + pieces + [zero_rows], axis=0).astype(BF16)
        return carry

    lax.fori_loop(0, nc, prologue, 0, unroll=2)

    sf_scr[...] = jnp.zeros_like(sf_scr)
    sb_scr[...] = jnp.zeros_like(sb_scr)

    def chunk_rows(c):
        return pl.ds(pl.multiple_of(c * L, L), L)

    def state_inputs(c, expand, edge_row):
        rows = chunk_rows(c)
        expc = exp_scr[rows, :]
        dec = _dot(expc.astype(BF16), expand)
        xw = (xs_ref[0, rows, :].astype(F32) * _dot(w_scr[rows, :].astype(BF16), expand)).astype(BF16)
        ds = lax.dot_general(bm_ref[0, rows, :], xw, (((0,), (0,)), ((), ())), preferred_element_type=F32)
        return dec, ds, _dot_split(expc[edge_row:edge_row + 1, :], expand)

    def state_step(c, s_scr, dec, ds, total):
        y_off = _dot(cm_ref[0, chunk_rows(c), :], s_scr[...].astype(BF16)) * dec
        s_scr[...] = s_scr[...] * total + ds
        return y_off

    def fwd_pair(i, carry):
        chunks = (2 * i, 2 * i + 1)
        cbm, args = [], []
        for c in chunks:
            rows = chunk_rows(c)
            cbm.append(lax.dot_general(cm_ref[0, rows, :], bm_ref[0, rows, :], (((1,), (1,)), ((), ())),
                                       preferred_element_type=F32))
            qside = a_scr[rows, :]
            kside = rhs_scr[pl.ds(pl.multiple_of(c * LANES, LANES), LANES), :]
            args.append([_dot(jnp.where(head_lanes[j], qside, jnp.zeros_like(qside)), kside)
                         for j in range(hpg)])
        ins = [state_inputs(c, ef, L - 1) for c in chunks]
        y_off = [state_step(c, sf_scr, *ins[k]) for k, c in enumerate(chunks)]
        for k, c in enumerate(chunks):
            rows = chunk_rows(c)
            xs = xs_ref[0, rows, :]
            diag = row_scr[pl.ds(pl.multiple_of(c * hpg, hpg), hpg), :]
            pairs = []
            for jp in range(hpg // 2):
                ws = []
                for j in (2 * jp, 2 * jp + 1):
                    arg = jnp.where(below, args[k][j][:, :L],
                                    jnp.where(above, args[k][j][:, L:], diag[j:j + 1, :]))
                    ws.append((cbm[k] * jnp.exp2(arg)).astype(BF16))
                xp = xs[:, jp * LANES:(jp + 1) * LANES]
                zero = jnp.zeros_like(xp)
                rhs = jnp.concatenate([jnp.where(half, xp, zero), jnp.where(half, zero, xp)], axis=0)
                pairs.append(_dot(jnp.concatenate(ws, axis=1), rhs))
            y_scr[rows, :] = jnp.concatenate(pairs, axis=1) + y_off[k]
        return carry

    lax.fori_loop(0, nc // 2, fwd_pair, 0)

    dskip = dskip_ref[...]
    gnorm = norm_ref[...]

    def bwd_pair(i, carry):
        chunks = (nc - 1 - 2 * i, nc - 2 - 2 * i)
        ins = [state_inputs(c, eb, 0) for c in chunks]
        for k, c in enumerate(chunks):
            rows = chunk_rows(c)
            y_off = state_step(c, sb_scr, *ins[k])
            xf = xs_ref[0, rows, :].astype(F32)
            y = (y_scr[rows, :] + y_off + dskip * xf) * _silu(z_ref[0, rows, :].astype(F32))
            y = y * lax.rsqrt(jnp.mean(y * y, axis=-1, keepdims=True) + EPS) * gnorm
            o_ref[0, rows, :] = y.astype(o_ref.dtype)
        return carry

    lax.fori_loop(0, nc // 2, bwd_pair, 0)


def _dot_split_left(t, a):
    hi = a.astype(BF16)
    lo = (a - hi.astype(F32)).astype(BF16)
    return _dot(t, hi) + _dot(t, lo)


def _ssd(xc, z, dt, bias, alog, dskip, gnorm, ef, eb):
    b, s, _ = xc.shape
    nb = D_INNER // LANES
    grp = lambda n: pl.BlockSpec((1, s, n), lambda i, g: (i, 0, g))
    return pl.pallas_call(
        _ssd_kernel,
        out_shape=jax.ShapeDtypeStruct((b, s, D_INNER), BF16),
        grid=(b, SSM_GROUPS),
        in_specs=[grp(GROUP_DIM),
                  pl.BlockSpec((1, s, D_STATE), lambda i, g: (i, 0, nb + g)),
                  pl.BlockSpec((1, s, D_STATE), lambda i, g: (i, 0, nb + SSM_GROUPS + g)),
                  grp(GROUP_DIM), grp(LANES),
                  pl.BlockSpec((1, 1, LANES), lambda i, g: (g, 0, 0)),
                  pl.BlockSpec((1, 1, LANES), lambda i, g: (g, 0, 0)),
                  pl.BlockSpec((1, GROUP_DIM), lambda i, g: (0, g)),
                  pl.BlockSpec((1, GROUP_DIM), lambda i, g: (0, g)),
                  _const_spec(ef.shape), _const_spec(eb.shape)],
        out_specs=grp(GROUP_DIM),
        scratch_shapes=[pltpu.VMEM((s, GROUP_DIM), F32), pltpu.VMEM((s, LANES), BF16),
                        pltpu.VMEM((s, LANES), F32), pltpu.VMEM((s, LANES), F32),
                        pltpu.VMEM((s // CHUNK * HEADS_PER_GROUP, CHUNK), F32),
                        pltpu.VMEM((s // CHUNK * LANES, 2 * CHUNK), BF16),
                        pltpu.VMEM((D_STATE, GROUP_DIM), F32), pltpu.VMEM((D_STATE, GROUP_DIM), F32)],
        compiler_params=_params(("parallel", "parallel")),
        name="ssd",
    )(xc, xc, xc, z, dt, bias, alog, dskip, gnorm, ef, eb)


def _merge_kernel(x_ref, a_ref, m_ref, ga_ref, gb_ref, pa, pb, wo, o_ref):
    ba = _dot(a_ref[0], pa[...])
    bm = _dot(m_ref[0], pb[...])
    merged = (jax.nn.sigmoid(ga_ref[0].astype(F32)) * ba
              + jax.nn.sigmoid(gb_ref[0].astype(F32)) * bm)
    o_ref[0] = x_ref[0] + _dot(merged.astype(BF16), wo[...])


def _merge(x, a, m, ga, gb, pa, pb, wo, tm):
    b, s, d = x.shape
    tok = lambda n: pl.BlockSpec((1, tm, n), lambda i, j: (i, j, 0))
    return pl.pallas_call(
        _merge_kernel,
        out_shape=jax.ShapeDtypeStruct(x.shape, F32),
        grid=(b, s // tm),
        in_specs=[tok(d), tok(a.shape[2]), tok(m.shape[2]), tok(d), tok(d),
                  _const_spec(pa.shape), _const_spec(pb.shape), _const_spec(wo.shape)],
        out_specs=tok(d),
        compiler_params=_params(("parallel", "parallel")),
        name="merge",
    )(x, a, m, ga, gb, pa, pb, wo)


def _pad_heads(w, width):
    k = w.shape[0]
    w = w.reshape(k, N_HEADS_MLA, width)
    return jnp.pad(w, ((0, 0), (0, 0), (0, HEAD_SLOT - width))).reshape(k, N_HEADS_MLA * HEAD_SLOT)


def _group_lanes(v_f, v_b, fill):
    f = v_f.reshape(SSM_GROUPS, HEADS_PER_GROUP)
    bk = v_b.reshape(SSM_GROUPS, HEADS_PER_GROUP)
    rest = jnp.full((SSM_GROUPS, LANES - 2 * HEADS_PER_GROUP), fill, F32)
    return jnp.concatenate([f, bk, rest], axis=1).reshape(SSM_GROUPS, 1, LANES)


def _expanders():
    ch = np.arange(GROUP_DIM) // SSM_HEAD_DIM
    row = np.arange(LANES)[:, None]
    ef = (row == ch[None, :]).astype(np.float32)
    eb = (row == ch[None, :] + HEADS_PER_GROUP).astype(np.float32)
    return jnp.asarray(ef, BF16), jnp.asarray(eb, BF16)


_ROPE_SWAP = np.concatenate([np.arange(QK_NOPE), np.arange(QK_NOPE + QK_ROPE // 2, QK_HEAD),
                             np.arange(QK_NOPE, QK_NOPE + QK_ROPE // 2), np.arange(QK_HEAD, HEAD_SLOT)])


def _rope_lanes():
    inv_freq = 1.0 / (ROPE_BASE ** (jnp.arange(0, QK_ROPE, 2, dtype=F32) / QK_ROPE))
    half = QK_ROPE // 2
    zeros = jnp.zeros((QK_NOPE,), F32)
    tail = jnp.zeros((HEAD_SLOT - QK_HEAD,), F32)
    invf = jnp.concatenate([zeros, inv_freq, inv_freq, tail]).reshape(1, HEAD_SLOT)
    sgn = jnp.concatenate([zeros, -jnp.ones((half,), F32), jnp.ones((half,), F32), tail])
    return invf, sgn.reshape(1, HEAD_SLOT)


def kernel(x, positions, ffn1_norm, ffn1_w_gate, ffn1_w_up, ffn1_w_down, mix_norm, w_in, q_a_norm, w_q_b, kv_a_norm, w_kv_b, q_head_norm, k_head_norm, conv_w, conv_b, a_log_fwd, a_log_bwd, dt_bias_fwd, dt_bias_bwd, d_skip, ssm_norm, w_attn_branch, w_ssm_branch, w_out, ffn2_norm, ffn2_w_gate, ffn2_w_up, ffn2_w_down):
    b, s, d = x.shape
    depth = ffn1_norm.shape[0]
    tm = min(512, s)
    bf = lambda w: w.astype(BF16)
    row = lambda v: v.reshape(1, -1).astype(F32)
    ef, eb = _expanders()
    invf, sgn = _rope_lanes()
    pos = positions.reshape(b, s, 1).astype(jnp.int32)

    for l in range(depth):
        x = _ffn(x, row(ffn1_norm[l]), bf(ffn1_w_gate[l]), bf(ffn1_w_up[l]), bf(ffn1_w_down[l]), tm)

        w = w_in[l]
        o = np.cumsum((0, Q_LORA, KV_LORA, QK_ROPE, D_INNER, XBC_DIM, SSM_HEADS, SSM_HEADS,
                       D_MODEL, D_MODEL))
        seg = [w[:, o[i]:o[i + 1]] for i in range(9)]
        w_kpe = jnp.pad(seg[2], ((0, 0), (QK_NOPE, HEAD_SLOT - QK_HEAD)))
        dtf = seg[5].reshape(d, SSM_GROUPS, HEADS_PER_GROUP)
        dtb = seg[6].reshape(d, SSM_GROUPS, HEADS_PER_GROUP)
        w_dt = jnp.pad(jnp.concatenate([dtf, dtb], axis=2),
                       ((0, 0), (0, 0), (0, LANES - 2 * HEADS_PER_GROUP))).reshape(d, SSM_GROUPS * LANES)
        ws = [bf(seg[0]), bf(seg[1]), bf(w_kpe), bf(seg[3]), bf(seg[4]), bf(w_dt), bf(seg[7]), bf(seg[8])]
        cq, ckv, kpe, z, xbc, dt, ga, gb = _inproj(x, row(mix_norm[l]), ws, tm)

        scale = math.log2(math.e) / math.sqrt(QK_HEAD)
        wkv = w_kv_b[l].reshape(KV_LORA, N_HEADS_MLA, QK_NOPE + V_HEAD)
        w_k = _pad_heads(wkv[:, :, :QK_NOPE].reshape(KV_LORA, N_HEADS_MLA * QK_NOPE), QK_NOPE)
        w_v = wkv[:, :, QK_NOPE:].reshape(KV_LORA, N_HEADS_MLA * V_HEAD)
        pad_gain = lambda g: jnp.pad(g, (0, HEAD_SLOT - QK_HEAD)).reshape(1, HEAD_SLOT)
        w_q = _pad_heads(w_q_b[l], QK_HEAD)
        w_q_sw = w_q.reshape(Q_LORA, N_HEADS_MLA, HEAD_SLOT)[:, :, _ROPE_SWAP].reshape(w_q.shape)
        g_q, g_k = pad_gain(q_head_norm[l] * scale), pad_gain(k_head_norm[l])
        consts = [row(q_a_norm[l]), bf(w_q), bf(w_q_sw), row(kv_a_norm[l]), bf(w_k), bf(w_v),
                  g_q, g_q[:, _ROPE_SWAP], g_k, g_k[:, _ROPE_SWAP], invf, sgn]
        q, k, vt = _mlaprep(cq, ckv, kpe, pos, consts, tm)
        a = _flash(q, k, vt, tq=min(256, s), nq=2 if s >= 512 else 1, ts=tm)

        xc = _conv(xbc, conv_w[l].reshape(CONV_WIDTH, XBC_DIM), row(conv_b[l]))
        m = _ssd(xc, z, dt,
                 _group_lanes(dt_bias_fwd[l], dt_bias_bwd[l], 0.0),
                 _group_lanes(a_log_fwd[l], a_log_bwd[l], 0.0),
                 row(jnp.repeat(d_skip[l], SSM_HEAD_DIM)), row(ssm_norm[l]), ef, eb)

        x = _merge(x, a, m, ga, gb, bf(w_attn_branch[l]), bf(w_ssm_branch[l]), bf(w_out[l]), tm)
        x = _ffn(x, row(ffn2_norm[l]), bf(ffn2_w_gate[l]), bf(ffn2_w_up[l]), bf(ffn2_w_down[l]), tm)
    return x
```

```python
import functools
import math

import numpy as np
import jax
import jax.numpy as jnp
from jax import lax
from jax.experimental import pallas as pl
from jax.experimental.pallas import tpu as pltpu

F32 = jnp.float32
BF16 = jnp.bfloat16

D_MODEL = 1024
D_FF = 2816
EPS = 1e-6
N_HEADS_MLA = 16
QK_NOPE = 64
QK_ROPE = 32
QK_HEAD = QK_NOPE + QK_ROPE
V_HEAD = 64
Q_LORA = 384
KV_LORA = 256
ROPE_BASE = 10000.0
D_INNER = 2 * D_MODEL
SSM_HEAD_DIM = 64
SSM_HEADS = D_INNER // SSM_HEAD_DIM
SSM_GROUPS = 4
HEADS_PER_GROUP = SSM_HEADS // SSM_GROUPS
D_STATE = 128
CONV_WIDTH = 5
CHUNK = 128
XBC_DIM = D_INNER + 2 * SSM_GROUPS * D_STATE
GROUP_DIM = D_INNER // SSM_GROUPS

LANES = 128
HEAD_SLOT = LANES
VMEM_LIMIT_BYTES = 56 * 1024 * 1024


def _params(sem, vmem=VMEM_LIMIT_BYTES):
    return pltpu.CompilerParams(dimension_semantics=sem, vmem_limit_bytes=vmem)


def _const_spec(shape):
    nd = len(shape)
    return pl.BlockSpec(shape, lambda *_: (0,) * nd, pipeline_mode=pl.Buffered(1))


def _rms(x, g):
    return x * lax.rsqrt(jnp.mean(x * x, axis=-1, keepdims=True) + EPS) * g


def _silu(x):
    return x * jax.nn.sigmoid(x)


def _dot(a, b):
    return jnp.dot(a, b, preferred_element_type=F32)


def _split3(a):
    hi = a.astype(BF16).astype(F32)
    mid = (a - hi).astype(BF16).astype(F32)
    lo = (a - hi - mid).astype(BF16).astype(F32)
    return hi, mid, lo


def _dot_split(a, b):
    hi = a.astype(BF16)
    lo = (a - hi.astype(F32)).astype(BF16)
    return _dot(hi, b) + _dot(lo, b)


def _ffn_kernel(x_ref, g_ref, wg_ref, wu_ref, wd_ref, o_ref):
    x = x_ref[0]
    hb = _rms(x, g_ref[...]).astype(BF16)
    gate = _dot(hb, wg_ref[...])
    up = _dot(hb, wu_ref[...])
    act = (_silu(gate) * up).astype(BF16)
    o_ref[0] = x + 0.5 * _dot(act, wd_ref[...])


def _ffn(x, g, wg, wu, wd, tm):
    b, s, d = x.shape
    tok = pl.BlockSpec((1, tm, d), lambda i, j: (i, j, 0))
    return pl.pallas_call(
        _ffn_kernel,
        out_shape=jax.ShapeDtypeStruct(x.shape, F32),
        grid=(b, s // tm),
        in_specs=[tok, _const_spec(g.shape), _const_spec(wg.shape), _const_spec(wu.shape),
                  _const_spec(wd.shape)],
        out_specs=tok,
        compiler_params=_params(("parallel", "parallel")),
        name="ffn",
    )(x, g, wg, wu, wd)


def _inproj_kernel(x_ref, g_ref, wcq, wckv, wkpe, wz, wxbc, wdt, wga, wgb,
                   ocq, ockv, okpe, oz, oxbc, odt, oga, ogb):
    hb = _rms(x_ref[0], g_ref[...]).astype(BF16)
    ocq[0] = _dot(hb, wcq[...])
    ockv[0] = _dot(hb, wckv[...])
    okpe[0] = _dot(hb, wkpe[...])
    oz[0] = _dot(hb, wz[...]).astype(BF16)
    oxbc[0] = _dot(hb, wxbc[...]).astype(BF16)
    odt[0] = _dot(hb, wdt[...])
    oga[0] = _dot(hb, wga[...]).astype(BF16)
    ogb[0] = _dot(hb, wgb[...]).astype(BF16)


def _inproj(x, g, ws, tm):
    b, s, d = x.shape
    dts = (F32, F32, F32, BF16, BF16, F32, BF16, BF16)
    tok = lambda n: pl.BlockSpec((1, tm, n), lambda i, j: (i, j, 0))
    return pl.pallas_call(
        _inproj_kernel,
        out_shape=[jax.ShapeDtypeStruct((b, s, w.shape[1]), dt) for w, dt in zip(ws, dts)],
        grid=(b, s // tm),
        in_specs=[tok(d), _const_spec(g.shape)] + [_const_spec(w.shape) for w in ws],
        out_specs=[tok(w.shape[1]) for w in ws],
        compiler_params=_params(("parallel", "parallel")),
        name="inproj",
    )(x, g, *ws)


def _mlaprep_kernel(cq_ref, ckv_ref, kpe_ref, pos_ref, qan, wq, wq_sw, kvan, wk, wv, qhn, qhn_sw, khn, khn_sw,
                    invf, sgn, oq, ok, ov):
    qn = _rms(cq_ref[0], qan[...]).astype(BF16)
    kvn = _rms(ckv_ref[0], kvan[...]).astype(BF16)
    q_full = _dot(qn, wq[...])
    q_part = _dot(qn, wq_sw[...])
    k_full = _dot(kvn, wk[...])
    v_full = _dot(kvn, wv[...])
    ones = jnp.ones((HEAD_SLOT - V_HEAD, v_full.shape[0]), BF16)
    for hp in range(N_HEADS_MLA // 2):
        vt = v_full[:, hp * LANES:(hp + 1) * LANES].T.astype(BF16)
        for i in range(2):
            base = (2 * hp + i) * HEAD_SLOT
            ov[0, 0, base:base + V_HEAD, :] = vt[i * V_HEAD:(i + 1) * V_HEAD, :]
            ov[0, 0, base + V_HEAD:base + HEAD_SLOT, :] = ones
    kpe = kpe_ref[0]
    ang = pos_ref[0].astype(F32) * invf[...]
    cos = jnp.cos(ang)
    sin = jnp.sin(ang) * sgn[...]
    lane = lax.broadcasted_iota(jnp.int32, cos.shape, 1)
    kpe_part = jnp.where(lane < QK_NOPE + QK_ROPE // 2,
                         pltpu.roll(kpe, LANES - QK_ROPE // 2, axis=1),
                         pltpu.roll(kpe, QK_ROPE // 2, axis=1))
    q_cos, q_sin = qhn[...] * cos, qhn_sw[...] * sin
    k_cos, k_sin = khn[...] * cos, khn_sw[...] * sin
    k_rot = kpe_part * k_sin

    def inv_rms(t):
        return lax.rsqrt(jnp.sum(t * t, axis=-1, keepdims=True) * (1.0 / QK_HEAD) + EPS)

    for h in range(N_HEADS_MLA):
        sl = slice(h * HEAD_SLOT, (h + 1) * HEAD_SLOT)
        qh = q_full[:, sl]
        oq[0, :, sl] = ((qh * q_cos + q_part[:, sl] * q_sin) * inv_rms(qh)).astype(BF16)
        kh = k_full[:, sl] + kpe
        ok[0, :, sl] = ((kh * k_cos + k_rot) * inv_rms(kh)).astype(BF16)


def _mlaprep(cq, ckv, kpe, pos, consts, tm):
    b, s, _ = cq.shape
    tok = lambda n: pl.BlockSpec((1, tm, n), lambda i, j: (i, j, 0))
    hw = N_HEADS_MLA * HEAD_SLOT
    return pl.pallas_call(
        _mlaprep_kernel,
        out_shape=[jax.ShapeDtypeStruct((b, s, hw), BF16), jax.ShapeDtypeStruct((b, s, hw), BF16),
                   jax.ShapeDtypeStruct((b, s // tm, hw, tm), BF16)],
        grid=(b, s // tm),
        in_specs=[tok(Q_LORA), tok(KV_LORA), tok(LANES), tok(1)] + [_const_spec(c.shape) for c in consts],
        out_specs=[tok(hw), tok(hw), pl.BlockSpec((1, 1, hw, tm), lambda i, j: (i, j, 0, 0))],
        compiler_params=_params(("parallel", "parallel")),
        name="mlaprep",
    )(cq, ckv, kpe, pos, *consts)


def _flash_kernel(q_ref, k_ref, vt_ref, o_ref, sta_scr, stb_scr, *, tq):
    nk, _, tk = vt_ref.shape[1:]
    nq = q_ref.shape[1] // tq
    heads = (slice(0, HEAD_SLOT), slice(HEAD_SLOT, 2 * HEAD_SLOT))

    def tile_rows(i):
        return pl.ds(pl.multiple_of(i * tq, tq), tq)

    def score_stage(qs, buf, j, mx):
        out = []
        for h, sl in enumerate(heads):
            st = lax.dot_general(k_ref[0, j * tk:(j + 1) * tk, sl], qs[h], (((1,), (1,)), ((), ())),
                                 preferred_element_type=F32)
            buf[h, j * tk:(j + 1) * tk, :] = st
            out.append(jnp.maximum(mx[h], jnp.max(st, axis=0, keepdims=True)))
        return out

    def value_stage(buf, j, mx, accs):
        out = []
        for h, sl in enumerate(heads):
            pt = jnp.exp2(buf[h, j * tk:(j + 1) * tk, :] - mx[h]).astype(BF16)
            out.append(accs[h] + _dot(vt_ref[0, j, sl, :], pt))
        return out

    def finish(i, accs):
        ot = jnp.concatenate([acc[0:V_HEAD] / acc[V_HEAD:V_HEAD + 1] for acc in accs], axis=0)
        o_ref[0, tile_rows(i), :] = ot.T.astype(o_ref.dtype)

    neg_inf = [jnp.full((1, tq), -jnp.inf, F32)] * 2
    zeros = [jnp.zeros((HEAD_SLOT, tq), F32)] * 2

    def scores_only(i, buf):
        qs = [q_ref[0, tile_rows(i), sl] for sl in heads]
        mx = neg_inf
        for j in range(nk):
            mx = score_stage(qs, buf, j, mx)
        return mx

    def values_only(i, buf, mx):
        accs = zeros
        for j in range(nk):
            accs = value_stage(buf, j, mx, accs)
        finish(i, accs)

    def overlapped(i, buf, mx, buf_next):
        qs = [q_ref[0, tile_rows(i + 1), sl] for sl in heads]
        mx_next, accs = neg_inf, zeros
        for j in range(nk):
            mx_next = score_stage(qs, buf_next, j, mx_next)
            accs = value_stage(buf, j, mx, accs)
        finish(i, accs)
        return mx_next

    def tile_pair(p, mx_a):
        mx_b = overlapped(2 * p, sta_scr, list(mx_a), stb_scr)
        return tuple(overlapped(2 * p + 1, stb_scr, mx_b, sta_scr))

    mx_a = lax.fori_loop(0, nq // 2 - 1, tile_pair, tuple(scores_only(0, sta_scr)))
    mx_b = overlapped(nq - 2, sta_scr, list(mx_a), stb_scr)
    values_only(nq - 1, stb_scr, mx_b)


def _flash(q, k, vt, tq):
    b, s, _ = q.shape
    nk, _, tk = vt.shape[1:]
    return pl.pallas_call(
        functools.partial(_flash_kernel, tq=tq),
        out_shape=jax.ShapeDtypeStruct((b, s, N_HEADS_MLA * V_HEAD), BF16),
        grid=(b, N_HEADS_MLA // 2),
        in_specs=[pl.BlockSpec((1, s, 2 * HEAD_SLOT), lambda i, p: (i, 0, p)),
                  pl.BlockSpec((1, s, 2 * HEAD_SLOT), lambda i, p: (i, 0, p)),
                  pl.BlockSpec((1, nk, 2 * HEAD_SLOT, tk), lambda i, p: (i, 0, p, 0))],
        out_specs=pl.BlockSpec((1, s, 2 * V_HEAD), lambda i, p: (i, 0, p)),
        scratch_shapes=[pltpu.VMEM((2, s, tq), F32), pltpu.VMEM((2, s, tq), F32)],
        compiler_params=_params(("parallel", "parallel")),
        name="flash",
    )(q, k, vt)


CONV_HALO = 16
CONV_WINDOW = 256
CONV_ROWS = CONV_WINDOW - 2 * CONV_HALO


def _conv_kernel(x_ref, w_ref, b_ref, sh_ref, o_ref, pad_ref):
    s, n = x_ref.shape[1:]
    zeros = jnp.zeros((CONV_HALO, n), pad_ref.dtype)
    pad_ref[0:CONV_HALO, :] = zeros
    pad_ref[CONV_HALO + s:2 * CONV_HALO + s, :] = zeros
    pad_ref[CONV_HALO:CONV_HALO + s, :] = x_ref[0]
    w = w_ref[...]
    bias = jnp.broadcast_to(b_ref[...], (CONV_ROWS, n))
    for c in range(-(-s // CONV_ROWS)):
        r0 = min(c * CONV_ROWS, s - CONV_ROWS)
        win = pad_ref[r0:r0 + CONV_WINDOW, :]
        acc = bias
        for t in range(CONV_WIDTH):
            if t == CONV_WIDTH // 2:
                tap = win[CONV_HALO:CONV_HALO + CONV_ROWS, :].astype(F32)
            else:
                tap = _dot(sh_ref[t], win)
            acc = acc + w[t:t + 1, :] * tap
        o_ref[0, r0:r0 + CONV_ROWS, :] = _silu(acc).astype(o_ref.dtype)


def _conv_shifts():
    t = np.arange(CONV_WIDTH)[:, None, None]
    i = np.arange(CONV_ROWS)[None, :, None]
    j = np.arange(CONV_WINDOW)[None, None, :]
    return jnp.asarray(j == CONV_HALO + i + t - CONV_WIDTH // 2, BF16)


def _conv(xbc, w, bias):
    b, s, n = xbc.shape
    sh = _conv_shifts()
    return pl.pallas_call(
        _conv_kernel,
        out_shape=jax.ShapeDtypeStruct(xbc.shape, BF16),
        grid=(b, n // GROUP_DIM),
        in_specs=[pl.BlockSpec((1, s, GROUP_DIM), lambda i, c: (i, 0, c)),
                  pl.BlockSpec((CONV_WIDTH, GROUP_DIM), lambda i, c: (0, c)),
                  pl.BlockSpec((1, GROUP_DIM), lambda i, c: (0, c)),
                  _const_spec(sh.shape)],
        out_specs=pl.BlockSpec((1, s, GROUP_DIM), lambda i, c: (i, 0, c)),
        scratch_shapes=[pltpu.VMEM((s + 2 * CONV_HALO, GROUP_DIM), BF16)],
        compiler_params=_params(("parallel", "parallel")),
        name="conv",
    )(xbc, w, bias, sh)


def _ssd_kernel(xs_ref, bm_ref, cm_ref, z_ref, dt_ref, bias_ref, alog_ref, dskip_ref, norm_ref,
                ef_ref, eb_ref, o_ref, y_scr, a_scr, exp_scr, w_scr, row_scr, rhs_scr, sf_scr, sb_scr):
    L = CHUNK
    hpg = HEADS_PER_GROUP
    nl = 2 * hpg
    nc = xs_ref.shape[1] // L
    a2 = -jnp.exp(alog_ref[0]) * math.log2(math.e)
    bias = bias_ref[0]
    ef = ef_ref[...]
    eb = eb_ref[...]
    li = lax.broadcasted_iota(jnp.int32, (L, L), 0)
    si = lax.broadcasted_iota(jnp.int32, (L, L), 1)
    tril = (si <= li).astype(BF16)
    triu = (si >= li).astype(BF16)
    below = si < li
    above = si > li
    lane = lax.broadcasted_iota(jnp.int32, (L, LANES), 1)
    is_fwd_lane = lane < hpg
    half = lane < SSM_HEAD_DIM
    fwd_rows = lax.broadcasted_iota(jnp.int32, (nl, L), 0) < hpg
    ri = lax.broadcasted_iota(jnp.int32, (3 * nl, 2 * L), 0)
    ci = lax.broadcasted_iota(jnp.int32, (3 * nl, 2 * L), 1)
    ones_rows = ((ri % nl < hpg) == (ci < L)).astype(F32)
    zero_rows = jnp.zeros((nl, 2 * L), F32)
    head_lanes = [lane % hpg == j for j in range(hpg)]

    def prologue(c, carry):
        rows = pl.ds(pl.multiple_of(c * L, L), L)
        dtv = jax.nn.softplus(dt_ref[0, rows, :] + bias)
        da = dtv * a2
        cs = jnp.where(is_fwd_lane, _dot_split_left(tril, da), _dot_split_left(triu, da))
        tot = jnp.where(is_fwd_lane[0:1], cs[L - 1:L, :], cs[0:1, :])
        exp_scr[rows, :] = jnp.exp2(cs)
        w_scr[rows, :] = jnp.exp2(tot - cs) * dtv
        q_hi, q_mid, q_lo = _split3(cs)
        a_scr[rows, :] = jnp.where(
            lane < nl, q_hi, jnp.where(
                lane < 2 * nl, pltpu.roll(q_mid, nl, axis=1), jnp.where(
                    lane < 3 * nl, pltpu.roll(q_lo, 2 * nl, axis=1), jnp.where(
                        (lane >= LANES // 2) & (lane < LANES // 2 + 3 * nl), 1.0, 0.0)))).astype(BF16)
        both = jnp.log2(dtv + pltpu.roll(dtv, LANES - hpg, axis=1))
        packed = jnp.where(lane < nl, cs - jnp.log2(dtv), pltpu.roll(both, nl, axis=1))
        pt = packed.T
        row_scr[pl.ds(pl.multiple_of(c * hpg, hpg), hpg), :] = pt[nl:nl + hpg, :]
        pieces = []
        for piece in _split3(pt[0:nl, :]):
            pieces.append(jnp.concatenate([jnp.where(fwd_rows, -piece, 0.0),
                                           jnp.where(fwd_rows, 0.0, -piece)], axis=1))
        rhs_scr[pl.ds(pl.multiple_of(c * LANES, LANES), LANES), :] = jnp.concatenate(
            [ones_rows, zero_rows] + pieces + [zero_rows], axis=0).astype(BF16)
        return carry

    lax.fori_loop(0, nc, prologue, 0, unroll=2)

    sf_scr[...] = jnp.zeros_like(sf_scr)
    sb_scr[...] = jnp.zeros_like(sb_scr)

    def chunk_rows(c):
        return pl.ds(pl.multiple_of(c * L, L), L)

    def state_inputs(c, expand, edge_row):
        rows = chunk_rows(c)
        expc = exp_scr[rows, :]
        dec = _dot(expc.astype(BF16), expand)
        xw = (xs_ref[0, rows, :].astype(F32) * _dot(w_scr[rows, :].astype(BF16), expand)).astype(BF16)
        ds = lax.dot_general(bm_ref[0, rows, :], xw, (((0,), (0,)), ((), ())), preferred_element_type=F32)
        return dec, ds, _dot_split(expc[edge_row:edge_row + 1, :], expand)

    def state_step(c, s_scr, dec, ds, total):
        y_off = _dot(cm_ref[0, chunk_rows(c), :], s_scr[...].astype(BF16)) * dec
        s_scr[...] = s_scr[...] * total + ds
        return y_off

    def fwd_pair(i, carry):
        chunks = (2 * i, 2 * i + 1)
        cbm, args = [], []
        for c in chunks:
            rows = chunk_rows(c)
            cbm.append(lax.dot_general(cm_ref[0, rows, :], bm_ref[0, rows, :], (((1,), (1,)), ((), ())),
                                       preferred_element_type=F32))
            qside = a_scr[rows, :]
            kside = rhs_scr[pl.ds(pl.multiple_of(c * LANES, LANES), LANES), :]
            args.append([_dot(jnp.where(head_lanes[j], qside, jnp.zeros_like(qside)), kside)
                         for j in range(hpg)])
        ins = [state_inputs(c, ef, L - 1) for c in chunks]
        y_off = [state_step(c, sf_scr, *ins[k]) for k, c in enumerate(chunks)]
        for k, c in enumerate(chunks):
            rows = chunk_rows(c)
            xs = xs_ref[0, rows, :]
            diag = row_scr[pl.ds(pl.multiple_of(c * hpg, hpg), hpg), :]
            pairs = []
            for jp in range(hpg // 2):
                ws = []
                for j in (2 * jp, 2 * jp + 1):
                    arg = jnp.where(below, args[k][j][:, :L],
                                    jnp.where(above, args[k][j][:, L:], diag[j:j + 1, :]))
                    ws.append((cbm[k] * jnp.exp2(arg)).astype(BF16))
                xp = xs[:, jp * LANES:(jp + 1) * LANES]
                zero = jnp.zeros_like(xp)
                rhs = jnp.concatenate([jnp.where(half, xp, zero), jnp.where(half, zero, xp)], axis=0)
                pairs.append(_dot(jnp.concatenate(ws, axis=1), rhs))
            y_scr[rows, :] = jnp.concatenate(pairs, axis=1) + y_off[k]
        return carry

    lax.fori_loop(0, nc // 2, fwd_pair, 0)

    dskip = dskip_ref[...]
    gnorm = norm_ref[...]

    def bwd_pair(i, carry):
        chunks = (nc - 1 - 2 * i, nc - 2 - 2 * i)
        ins = [state_inputs(c, eb, 0) for c in chunks]
        for k, c in enumerate(chunks):
            rows = chunk_rows(c)
            y_off = state_step(c, sb_scr, *ins[k])
            xf = xs_ref[0, rows, :].astype(F32)
            y = (y_scr[rows, :] + y_off + dskip * xf) * _silu(z_ref[0, rows, :].astype(F32))
            y = y * lax.rsqrt(jnp.mean(y * y, axis=-1, keepdims=True) + EPS) * gnorm
            o_ref[0, rows, :] = y.astype(o_ref.dtype)
        return carry

    lax.fori_loop(0, nc // 2, bwd_pair, 0)


def _dot_split_left(t, a):
    hi = a.astype(BF16)
    lo = (a - hi.astype(F32)).astype(BF16)
    return _dot(t, hi) + _dot(t, lo)


def _ssd(xc, z, dt, bias, alog, dskip, gnorm, ef, eb):
    b, s, _ = xc.shape
    nb = D_INNER // LANES
    grp = lambda n: pl.BlockSpec((1, s, n), lambda i, g: (i, 0, g))
    return pl.pallas_call(
        _ssd_kernel,
        out_shape=jax.ShapeDtypeStruct((b, s, D_INNER), BF16),
        grid=(b, SSM_GROUPS),
        in_specs=[grp(GROUP_DIM),
                  pl.BlockSpec((1, s, D_STATE), lambda i, g: (i, 0, nb + g)),
                  pl.BlockSpec((1, s, D_STATE), lambda i, g: (i, 0, nb + SSM_GROUPS + g)),
                  grp(GROUP_DIM), grp(LANES),
                  pl.BlockSpec((1, 1, LANES), lambda i, g: (g, 0, 0)),
                  pl.BlockSpec((1, 1, LANES), lambda i, g: (g, 0, 0)),
                  pl.BlockSpec((1, GROUP_DIM), lambda i, g: (0, g)),
                  pl.BlockSpec((1, GROUP_DIM), lambda i, g: (0, g)),
                  _const_spec(ef.shape), _const_spec(eb.shape)],
        out_specs=grp(GROUP_DIM),
        scratch_shapes=[pltpu.VMEM((s, GROUP_DIM), F32), pltpu.VMEM((s, LANES), BF16),
                        pltpu.VMEM((s, LANES), F32), pltpu.VMEM((s, LANES), F32),
                        pltpu.VMEM((s // CHUNK * HEADS_PER_GROUP, CHUNK), F32),
                        pltpu.VMEM((s // CHUNK * LANES, 2 * CHUNK), BF16),
                        pltpu.VMEM((D_STATE, GROUP_DIM), F32), pltpu.VMEM((D_STATE, GROUP_DIM), F32)],
        compiler_params=_params(("parallel", "parallel")),
        name="ssd",
    )(xc, xc, xc, z, dt, bias, alog, dskip, gnorm, ef, eb)


def _merge_kernel(x_ref, a_ref, m_ref, ga_ref, gb_ref, pa, pb, wo, o_ref):
    ba = _dot(a_ref[0], pa[...])
    bm = _dot(m_ref[0], pb[...])
    merged = (jax.nn.sigmoid(ga_ref[0].astype(F32)) * ba
              + jax.nn.sigmoid(gb_ref[0].astype(F32)) * bm)
    o_ref[0] = x_ref[0] + _dot(merged.astype(BF16), wo[...])


def _merge(x, a, m, ga, gb, pa, pb, wo, tm):
    b, s, d = x.shape
    tok = lambda n: pl.BlockSpec((1, tm, n), lambda i, j: (i, j, 0))
    return pl.pallas_call(
        _merge_kernel,
        out_shape=jax.ShapeDtypeStruct(x.shape, F32),
        grid=(b, s // tm),
        in_specs=[tok(d), tok(a.shape[2]), tok(m.shape[2]), tok(d), tok(d),
                  _const_spec(pa.shape), _const_spec(pb.shape), _const_spec(wo.shape)],
        out_specs=tok(d),
        compiler_params=_params(("parallel", "parallel")),
        name="merge",
    )(x, a, m, ga, gb, pa, pb, wo)


def _pad_heads(w, width):
    k = w.shape[0]
    w = w.reshape(k, N_HEADS_MLA, width)
    return jnp.pad(w, ((0, 0), (0, 0), (0, HEAD_SLOT - width))).reshape(k, N_HEADS_MLA * HEAD_SLOT)


def _group_lanes(v_f, v_b, fill):
    f = v_f.reshape(SSM_GROUPS, HEADS_PER_GROUP)
    bk = v_b.reshape(SSM_GROUPS, HEADS_PER_GROUP)
    rest = jnp.full((SSM_GROUPS, LANES - 2 * HEADS_PER_GROUP), fill, F32)
    return jnp.concatenate([f, bk, rest], axis=1).reshape(SSM_GROUPS, 1, LANES)


def _expanders():
    ch = np.arange(GROUP_DIM) // SSM_HEAD_DIM
    row = np.arange(LANES)[:, None]
    ef = (row == ch[None, :]).astype(np.float32)
    eb = (row == ch[None, :] + HEADS_PER_GROUP).astype(np.float32)
    return jnp.asarray(ef, BF16), jnp.asarray(eb, BF16)


_ROPE_SWAP = np.concatenate([np.arange(QK_NOPE), np.arange(QK_NOPE + QK_ROPE // 2, QK_HEAD),
                             np.arange(QK_NOPE, QK_NOPE + QK_ROPE // 2), np.arange(QK_HEAD, HEAD_SLOT)])


def _rope_lanes():
    inv_freq = 1.0 / (ROPE_BASE ** (jnp.arange(0, QK_ROPE, 2, dtype=F32) / QK_ROPE))
    half = QK_ROPE // 2
    zeros = jnp.zeros((QK_NOPE,), F32)
    tail = jnp.zeros((HEAD_SLOT - QK_HEAD,), F32)
    invf = jnp.concatenate([zeros, inv_freq, inv_freq, tail]).reshape(1, HEAD_SLOT)
    sgn = jnp.concatenate([zeros, -jnp.ones((half,), F32), jnp.ones((half,), F32), tail])
    return invf, sgn.reshape(1, HEAD_SLOT)


def kernel(x, positions, ffn1_norm, ffn1_w_gate, ffn1_w_up, ffn1_w_down, mix_norm, w_in, q_a_norm, w_q_b, kv_a_norm, w_kv_b, q_head_norm, k_head_norm, conv_w, conv_b, a_log_fwd, a_log_bwd, dt_bias_fwd, dt_bias_bwd, d_skip, ssm_norm, w_attn_branch, w_ssm_branch, w_out, ffn2_norm, ffn2_w_gate, ffn2_w_up, ffn2_w_down):
    b, s, d = x.shape
    depth = ffn1_norm.shape[0]
    tm = min(512, s)
    bf = lambda w: w.astype(BF16)
    row = lambda v: v.reshape(1, -1).astype(F32)
    ef, eb = _expanders()
    invf, sgn = _rope_lanes()
    pos = positions.reshape(b, s, 1).astype(jnp.int32)

    for l in range(depth):
        x = _ffn(x, row(ffn1_norm[l]), bf(ffn1_w_gate[l]), bf(ffn1_w_up[l]), bf(ffn1_w_down[l]), tm)

        w = w_in[l]
        o = np.cumsum((0, Q_LORA, KV_LORA, QK_ROPE, D_INNER, XBC_DIM, SSM_HEADS, SSM_HEADS,
                       D_MODEL, D_MODEL))
        seg = [w[:, o[i]:o[i + 1]] for i in range(9)]
        w_kpe = jnp.pad(seg[2], ((0, 0), (QK_NOPE, HEAD_SLOT - QK_HEAD)))
        dtf = seg[5].reshape(d, SSM_GROUPS, HEADS_PER_GROUP)
        dtb = seg[6].reshape(d, SSM_GROUPS, HEADS_PER_GROUP)
        w_dt = jnp.pad(jnp.concatenate([dtf, dtb], axis=2),
                       ((0, 0), (0, 0), (0, LANES - 2 * HEADS_PER_GROUP))).reshape(d, SSM_GROUPS * LANES)
        ws = [bf(seg[0]), bf(seg[1]), bf(w_kpe), bf(seg[3]), bf(seg[4]), bf(w_dt), bf(seg[7]), bf(seg[8])]
        cq, ckv, kpe, z, xbc, dt, ga, gb = _inproj(x, row(mix_norm[l]), ws, tm)

        scale = math.log2(math.e) / math.sqrt(QK_HEAD)
        wkv = w_kv_b[l].reshape(KV_LORA, N_HEADS_MLA, QK_NOPE + V_HEAD)
        w_k = _pad_heads(wkv[:, :, :QK_NOPE].reshape(KV_LORA, N_HEADS_MLA * QK_NOPE), QK_NOPE)
        w_v = wkv[:, :, QK_NOPE:].reshape(KV_LORA, N_HEADS_MLA * V_HEAD)
        pad_gain = lambda g: jnp.pad(g, (0, HEAD_SLOT - QK_HEAD)).reshape(1, HEAD_SLOT)
        w_q = _pad_heads(w_q_b[l], QK_HEAD)
        w_q_sw = w_q.reshape(Q_LORA, N_HEADS_MLA, HEAD_SLOT)[:, :, _ROPE_SWAP].reshape(w_q.shape)
        g_q, g_k = pad_gain(q_head_norm[l] * scale), pad_gain(k_head_norm[l])
        consts = [row(q_a_norm[l]), bf(w_q), bf(w_q_sw), row(kv_a_norm[l]), bf(w_k), bf(w_v),
                  g_q, g_q[:, _ROPE_SWAP], g_k, g_k[:, _ROPE_SWAP], invf, sgn]
        q, k, vt = _mlaprep(cq, ckv, kpe, pos, consts, tm)
        a = _flash(q, k, vt, tq=min(256, s))

        xc = _conv(xbc, conv_w[l].reshape(CONV_WIDTH, XBC_DIM), row(conv_b[l]))
        m = _ssd(xc, z, dt,
                 _group_lanes(dt_bias_fwd[l], dt_bias_bwd[l], 0.0),
                 _group_lanes(a_log_fwd[l], a_log_bwd[l], 0.0),
                 row(jnp.repeat(d_skip[l], SSM_HEAD_DIM)), row(ssm_norm[l]), ef, eb)

        x = _merge(x, a, m, ga, gb, bf(w_attn_branch[l]), bf(w_ssm_branch[l]), bf(w_out[l]), tm)
        x = _ffn(x, row(ffn2_norm[l]), bf(ffn2_w_gate[l]), bf(ffn2_w_up[l]), bf(ffn2_w_down[l]), tm)
    return x
```

```python
import functools
import math

import numpy as np
import jax
import jax.numpy as jnp
from jax import lax
from jax.experimental import pallas as pl
from jax.experimental.pallas import tpu as pltpu

F32 = jnp.float32
BF16 = jnp.bfloat16

D_MODEL = 1024
D_FF = 2816
EPS = 1e-6
N_HEADS_MLA = 16
QK_NOPE = 64
QK_ROPE = 32
QK_HEAD = QK_NOPE + QK_ROPE
V_HEAD = 64
Q_LORA = 384
KV_LORA = 256
ROPE_BASE = 10000.0
D_INNER = 2 * D_MODEL
SSM_HEAD_DIM = 64
SSM_HEADS = D_INNER // SSM_HEAD_DIM
SSM_GROUPS = 4
HEADS_PER_GROUP = SSM_HEADS // SSM_GROUPS
D_STATE = 128
CONV_WIDTH = 5
CHUNK = 128
XBC_DIM = D_INNER + 2 * SSM_GROUPS * D_STATE
GROUP_DIM = D_INNER // SSM_GROUPS

LANES = 128
HEAD_SLOT = LANES
V_SLOT = V_HEAD + 16
VMEM_LIMIT_BYTES = 56 * 1024 * 1024


def _params(sem, vmem=VMEM_LIMIT_BYTES):
    return pltpu.CompilerParams(dimension_semantics=sem, vmem_limit_bytes=vmem)


def _const_spec(shape):
    nd = len(shape)
    return pl.BlockSpec(shape, lambda *_: (0,) * nd, pipeline_mode=pl.Buffered(1))


def _rms(x, g):
    return x * lax.rsqrt(jnp.mean(x * x, axis=-1, keepdims=True) + EPS) * g


def _silu(x):
    return x * jax.nn.sigmoid(x)


def _dot(a, b):
    return jnp.dot(a, b, preferred_element_type=F32)


def _split3(a):
    hi = a.astype(BF16).astype(F32)
    mid = (a - hi).astype(BF16).astype(F32)
    lo = (a - hi - mid).astype(BF16).astype(F32)
    return hi, mid, lo


def _dot_split(a, b):
    hi = a.astype(BF16)
    lo = (a - hi.astype(F32)).astype(BF16)
    return _dot(hi, b) + _dot(lo, b)


def _ffn_kernel(x_ref, g_ref, wg_ref, wu_ref, wd_ref, o_ref):
    x = x_ref[0]
    hb = _rms(x, g_ref[...]).astype(BF16)
    gate = _dot(hb, wg_ref[...])
    up = _dot(hb, wu_ref[...])
    act = (_silu(gate) * up).astype(BF16)
    o_ref[0] = x + 0.5 * _dot(act, wd_ref[...])


def _ffn(x, g, wg, wu, wd, tm):
    b, s, d = x.shape
    tok = pl.BlockSpec((1, tm, d), lambda i, j: (i, j, 0))
    return pl.pallas_call(
        _ffn_kernel,
        out_shape=jax.ShapeDtypeStruct(x.shape, F32),
        grid=(b, s // tm),
        in_specs=[tok, _const_spec(g.shape), _const_spec(wg.shape), _const_spec(wu.shape),
                  _const_spec(wd.shape)],
        out_specs=tok,
        compiler_params=_params(("parallel", "parallel")),
        name="ffn",
    )(x, g, wg, wu, wd)


def _inproj_kernel(x_ref, g_ref, wcq, wckv, wkpe, wz, wxbc, wdt, wga, wgb,
                   ocq, ockv, okpe, oz, oxbc, odt, oga, ogb):
    hb = _rms(x_ref[0], g_ref[...]).astype(BF16)
    ocq[0] = _dot(hb, wcq[...])
    ockv[0] = _dot(hb, wckv[...])
    okpe[0] = _dot(hb, wkpe[...])
    oz[0] = _dot(hb, wz[...]).astype(BF16)
    oxbc[0] = _dot(hb, wxbc[...]).astype(BF16)
    odt[0] = _dot(hb, wdt[...])
    oga[0] = _dot(hb, wga[...]).astype(BF16)
    ogb[0] = _dot(hb, wgb[...]).astype(BF16)


def _inproj(x, g, ws, tm):
    b, s, d = x.shape
    dts = (F32, F32, F32, BF16, BF16, F32, BF16, BF16)
    tok = lambda n: pl.BlockSpec((1, tm, n), lambda i, j: (i, j, 0))
    return pl.pallas_call(
        _inproj_kernel,
        out_shape=[jax.ShapeDtypeStruct((b, s, w.shape[1]), dt) for w, dt in zip(ws, dts)],
        grid=(b, s // tm),
        in_specs=[tok(d), _const_spec(g.shape)] + [_const_spec(w.shape) for w in ws],
        out_specs=[tok(w.shape[1]) for w in ws],
        compiler_params=_params(("parallel", "parallel")),
        name="inproj",
    )(x, g, *ws)


def _mlaprep_kernel(cq_ref, ckv_ref, kpe_ref, pos_ref, qan, wq, wq_sw, kvan, wk, wv, qhn, qhn_sw, khn, khn_sw,
                    invf, sgn, oq, ok, ov):
    qn = _rms(cq_ref[0], qan[...]).astype(BF16)
    kvn = _rms(ckv_ref[0], kvan[...]).astype(BF16)
    q_full = _dot(qn, wq[...])
    q_part = _dot(qn, wq_sw[...])
    k_full = _dot(kvn, wk[...])
    v_full = _dot(kvn, wv[...])
    ones = jnp.ones((V_SLOT - V_HEAD, v_full.shape[0]), BF16)
    for hp in range(N_HEADS_MLA // 2):
        vt = v_full[:, hp * LANES:(hp + 1) * LANES].T.astype(BF16)
        for i in range(2):
            base = (2 * hp + i) * V_SLOT
            ov[0, 0, base:base + V_HEAD, :] = vt[i * V_HEAD:(i + 1) * V_HEAD, :]
            ov[0, 0, base + V_HEAD:base + V_SLOT, :] = ones
    kpe = kpe_ref[0]
    tm = kpe.shape[0]
    nblk = LANES // QK_ROPE
    quarter = tm // nblk
    pos = pos_ref[0].astype(F32)
    lane_q = lax.broadcasted_iota(jnp.int32, (quarter, LANES), 1)
    pos_p = pos[0:quarter]
    for i in range(1, nblk):
        pos_p = jnp.where(lane_q < i * QK_ROPE, pos_p, pos[i * quarter:(i + 1) * quarter])
    ang = pos_p * invf[...]
    cos_p = jnp.cos(ang)
    sin_p = jnp.sin(ang) * sgn[...]
    rope_lanes = (lane_q >= QK_NOPE) & (lane_q < QK_HEAD)

    def unpack(table, fill):
        parts = []
        for i in range(nblk):
            shift = (QK_NOPE - i * QK_ROPE) % LANES
            moved = pltpu.roll(table, shift, axis=1) if shift else table
            parts.append(jnp.where(rope_lanes, moved, fill))
        return jnp.concatenate(parts, axis=0)

    cos = unpack(cos_p, 1.0)
    sin = unpack(sin_p, 0.0)
    lane = lax.broadcasted_iota(jnp.int32, cos.shape, 1)
    kpe_part = jnp.where(lane < QK_NOPE + QK_ROPE // 2,
                         pltpu.roll(kpe, LANES - QK_ROPE // 2, axis=1),
                         pltpu.roll(kpe, QK_ROPE // 2, axis=1))
    q_cos, q_sin = qhn[...] * cos, qhn_sw[...] * sin
    k_cos, k_sin = khn[...] * cos, khn_sw[...] * sin
    k_rot = kpe_part * k_sin

    def inv_rms(t):
        return lax.rsqrt(jnp.sum(t * t, axis=-1, keepdims=True) * (1.0 / QK_HEAD) + EPS)

    for h in range(N_HEADS_MLA):
        sl = slice(h * HEAD_SLOT, (h + 1) * HEAD_SLOT)
        qh = q_full[:, sl]
        oq[0, :, sl] = ((qh * q_cos + q_part[:, sl] * q_sin) * inv_rms(qh)).astype(BF16)
        kh = k_full[:, sl] + kpe
        ok[0, :, sl] = ((kh * k_cos + k_rot) * inv_rms(kh)).astype(BF16)


def _mlaprep(cq, ckv, kpe, pos, consts, tm):
    b, s, _ = cq.shape
    tok = lambda n: pl.BlockSpec((1, tm, n), lambda i, j: (i, j, 0))
    hw = N_HEADS_MLA * HEAD_SLOT
    vw = N_HEADS_MLA * V_SLOT
    return pl.pallas_call(
        _mlaprep_kernel,
        out_shape=[jax.ShapeDtypeStruct((b, s, hw), BF16), jax.ShapeDtypeStruct((b, s, hw), BF16),
                   jax.ShapeDtypeStruct((b, s // tm, vw, tm), BF16)],
        grid=(b, s // tm),
        in_specs=[tok(Q_LORA), tok(KV_LORA), tok(LANES), tok(1)] + [_const_spec(c.shape) for c in consts],
        out_specs=[tok(hw), tok(hw), pl.BlockSpec((1, 1, vw, tm), lambda i, j: (i, j, 0, 0))],
        compiler_params=_params(("parallel", "parallel")),
        name="mlaprep",
    )(cq, ckv, kpe, pos, *consts)


def _flash_kernel(q_ref, k_ref, vt_ref, o_ref, sta_scr, stb_scr, *, tq):
    nk, _, tk = vt_ref.shape[1:]
    nq = q_ref.shape[1] // tq
    sk = tk
    ns = nk * tk // sk
    heads = (slice(0, HEAD_SLOT), slice(HEAD_SLOT, 2 * HEAD_SLOT))

    def tile_rows(i):
        return pl.ds(pl.multiple_of(i * tq, tq), tq)

    def score_stage(qs, buf, j, mx):
        out = []
        for h, sl in enumerate(heads):
            st = lax.dot_general(k_ref[0, j * sk:(j + 1) * sk, sl], qs[h], (((1,), (1,)), ((), ())),
                                 preferred_element_type=F32)
            buf[h, j * sk:(j + 1) * sk, :] = st
            out.append(jnp.maximum(mx[h], jnp.max(st, axis=0, keepdims=True)))
        return out

    def value_stage(buf, j, mx, accs):
        out = list(accs)
        for t in range(j * sk // tk, (j + 1) * sk // tk):
            for h, sl in enumerate(heads):
                pt = jnp.exp2(buf[h, t * tk:(t + 1) * tk, :] - mx[h]).astype(BF16)
                out[h] = out[h] + _dot(vt_ref[0, t, h * V_SLOT:(h + 1) * V_SLOT, :], pt)
        return out

    def finish(i, accs):
        ot = jnp.concatenate([acc[0:V_HEAD] / acc[V_HEAD:V_HEAD + 1] for acc in accs], axis=0)
        o_ref[0, tile_rows(i), :] = ot.T.astype(o_ref.dtype)

    neg_inf = [jnp.full((1, tq), -jnp.inf, F32)] * 2
    zeros = [jnp.zeros((V_SLOT, tq), F32)] * 2

    def scores_only(i, buf):
        qs = [q_ref[0, tile_rows(i), sl] for sl in heads]
        mx = neg_inf
        for j in range(ns):
            mx = score_stage(qs, buf, j, mx)
        return mx

    def values_only(i, buf, mx):
        accs = zeros
        for j in range(ns):
            accs = value_stage(buf, j, mx, accs)
        finish(i, accs)

    def overlapped(i, buf, mx, buf_next):
        qs = [q_ref[0, tile_rows(i + 1), sl] for sl in heads]
        mx_next, accs = neg_inf, zeros
        for j in range(ns):
            mx_next = score_stage(qs, buf_next, j, mx_next)
            accs = value_stage(buf, j, mx, accs)
        finish(i, accs)
        return mx_next

    def tile_pair(p, mx_a):
        mx_b = overlapped(2 * p, sta_scr, list(mx_a), stb_scr)
        return tuple(overlapped(2 * p + 1, stb_scr, mx_b, sta_scr))

    mx_a = lax.fori_loop(0, nq // 2 - 1, tile_pair, tuple(scores_only(0, sta_scr)))
    mx_b = overlapped(nq - 2, sta_scr, list(mx_a), stb_scr)
    values_only(nq - 1, stb_scr, mx_b)


def _flash(q, k, vt, tq):
    b, s, _ = q.shape
    nk, _, tk = vt.shape[1:]
    return pl.pallas_call(
        functools.partial(_flash_kernel, tq=tq),
        out_shape=jax.ShapeDtypeStruct((b, s, N_HEADS_MLA * V_HEAD), BF16),
        grid=(b, N_HEADS_MLA // 2),
        in_specs=[pl.BlockSpec((1, s, 2 * HEAD_SLOT), lambda i, p: (i, 0, p)),
                  pl.BlockSpec((1, s, 2 * HEAD_SLOT), lambda i, p: (i, 0, p)),
                  pl.BlockSpec((1, nk, 2 * V_SLOT, tk), lambda i, p: (i, 0, p, 0))],
        out_specs=pl.BlockSpec((1, s, 2 * V_HEAD), lambda i, p: (i, 0, p)),
        scratch_shapes=[pltpu.VMEM((2, s, tq), F32), pltpu.VMEM((2, s, tq), F32)],
        compiler_params=_params(("parallel", "parallel")),
        name="flash",
    )(q, k, vt)


CONV_HALO = 16
CONV_WINDOW = 256
CONV_ROWS = CONV_WINDOW - 2 * CONV_HALO


def _conv_kernel(x_ref, w_ref, b_ref, sh_ref, o_ref, pad_ref):
    s, n = x_ref.shape[1:]
    zeros = jnp.zeros((CONV_HALO, n), pad_ref.dtype)
    pad_ref[0:CONV_HALO, :] = zeros
    pad_ref[CONV_HALO + s:2 * CONV_HALO + s, :] = zeros
    pad_ref[CONV_HALO:CONV_HALO + s, :] = x_ref[0]
    w = w_ref[...]
    bias = jnp.broadcast_to(b_ref[...], (CONV_ROWS, n))
    for c in range(-(-s // CONV_ROWS)):
        r0 = min(c * CONV_ROWS, s - CONV_ROWS)
        win = pad_ref[r0:r0 + CONV_WINDOW, :]
        acc = bias
        for t in range(CONV_WIDTH):
            if t == CONV_WIDTH // 2:
                tap = win[CONV_HALO:CONV_HALO + CONV_ROWS, :].astype(F32)
            else:
                tap = _dot(sh_ref[t], win)
            acc = acc + w[t:t + 1, :] * tap
        o_ref[0, r0:r0 + CONV_ROWS, :] = _silu(acc).astype(o_ref.dtype)


def _conv_shifts():
    t = np.arange(CONV_WIDTH)[:, None, None]
    i = np.arange(CONV_ROWS)[None, :, None]
    j = np.arange(CONV_WINDOW)[None, None, :]
    return jnp.asarray(j == CONV_HALO + i + t - CONV_WIDTH // 2, BF16)


def _conv(xbc, w, bias):
    b, s, n = xbc.shape
    sh = _conv_shifts()
    return pl.pallas_call(
        _conv_kernel,
        out_shape=jax.ShapeDtypeStruct(xbc.shape, BF16),
        grid=(b, n // GROUP_DIM),
        in_specs=[pl.BlockSpec((1, s, GROUP_DIM), lambda i, c: (i, 0, c)),
                  pl.BlockSpec((CONV_WIDTH, GROUP_DIM), lambda i, c: (0, c)),
                  pl.BlockSpec((1, GROUP_DIM), lambda i, c: (0, c)),
                  _const_spec(sh.shape)],
        out_specs=pl.BlockSpec((1, s, GROUP_DIM), lambda i, c: (i, 0, c)),
        scratch_shapes=[pltpu.VMEM((s + 2 * CONV_HALO, GROUP_DIM), BF16)],
        compiler_params=_params(("parallel", "parallel")),
        name="conv",
    )(xbc, w, bias, sh)


def _ssd_kernel(xs_ref, bm_ref, cm_ref, z_ref, dt_ref, bias_ref, alog_ref, dskip_ref, norm_ref,
                ef_ref, eb_ref, o_ref, y_scr, a_scr, exp_scr, w_scr, row_scr, rhs_scr, sf_scr, sb_scr):
    L = CHUNK
    hpg = HEADS_PER_GROUP
    nl = 2 * hpg
    nc = xs_ref.shape[1] // L
    a2 = -jnp.exp(alog_ref[0]) * math.log2(math.e)
    bias = bias_ref[0]
    ef = ef_ref[...]
    eb = eb_ref[...]
    li = lax.broadcasted_iota(jnp.int32, (L, L), 0)
    si = lax.broadcasted_iota(jnp.int32, (L, L), 1)
    tril = (si <= li).astype(BF16)
    triu = (si >= li).astype(BF16)
    below = si < li
    above = si > li
    lane = lax.broadcasted_iota(jnp.int32, (L, LANES), 1)
    is_fwd_lane = lane < hpg
    half = lane < SSM_HEAD_DIM
    fwd_rows = lax.broadcasted_iota(jnp.int32, (nl, L), 0) < hpg
    ri = lax.broadcasted_iota(jnp.int32, (3 * nl, 2 * L), 0)
    ci = lax.broadcasted_iota(jnp.int32, (3 * nl, 2 * L), 1)
    ones_rows = ((ri % nl < hpg) == (ci < L)).astype(F32)
    zero_rows = jnp.zeros((nl, 2 * L), F32)
    head_lanes = [lane % hpg == j for j in range(hpg)]

    def prologue(c, carry):
        rows = pl.ds(pl.multiple_of(c * L, L), L)
        dtv = jax.nn.softplus(dt_ref[0, rows, :] + bias)
        da = dtv * a2
        cs = jnp.where(is_fwd_lane, _dot_split_left(tril, da), _dot_split_left(triu, da))
        tot = jnp.where(is_fwd_lane[0:1], cs[L - 1:L, :], cs[0:1, :])
        exp_scr[rows, :] = jnp.exp2(cs)
        w_scr[rows, :] = jnp.exp2(tot - cs) * dtv
        q_hi, q_mid, q_lo = _split3(cs)
        a_scr[rows, :] = jnp.where(
            lane < nl, q_hi, jnp.where(
                lane < 2 * nl, pltpu.roll(q_mid, nl, axis=1), jnp.where(
                    lane < 3 * nl, pltpu.roll(q_lo, 2 * nl, axis=1), jnp.where(
                        (lane >= LANES // 2) & (lane < LANES // 2 + 3 * nl), 1.0, 0.0)))).astype(BF16)
        both = jnp.log2(dtv + pltpu.roll(dtv, LANES - hpg, axis=1))
        packed = jnp.where(lane < nl, cs - jnp.log2(dtv), pltpu.roll(both, nl, axis=1))
        pt = packed.T
        row_scr[pl.ds(pl.multiple_of(c * hpg, hpg), hpg), :] = pt[nl:nl + hpg, :]
        pieces = []
        for piece in _split3(pt[0:nl, :]):
            pieces.append(jnp.concatenate([jnp.where(fwd_rows, -piece, 0.0),
                                           jnp.where(fwd_rows, 0.0, -piece)], axis=1))
        rhs_scr[pl.ds(pl.multiple_of(c * LANES, LANES), LANES), :] = jnp.concatenate(
            [ones_rows, zero_rows] + pieces + [zero_rows], axis=0).astype(BF16)
        return carry

    lax.fori_loop(0, nc, prologue, 0, unroll=4 if nc % 4 == 0 else 1)

    sf_scr[...] = jnp.zeros_like(sf_scr)
    sb_scr[...] = jnp.zeros_like(sb_scr)

    def chunk_rows(c):
        return pl.ds(pl.multiple_of(c * L, L), L)

    def state_inputs(chunks, expand, edge_row):
        decs, wexps, totals = [], [], []
        for c in chunks:
            rows = chunk_rows(c)
            expc = exp_scr[rows, :]
            decs.append(_dot(expc.astype(BF16), expand))
            wexps.append(_dot(w_scr[rows, :].astype(BF16), expand))
            totals.append(_dot_split(expc[edge_row:edge_row + 1, :], expand))
        out = []
        for k, c in enumerate(chunks):
            rows = chunk_rows(c)
            xw = (xs_ref[0, rows, :].astype(F32) * wexps[k]).astype(BF16)
            ds = lax.dot_general(bm_ref[0, rows, :], xw, (((0,), (0,)), ((), ())), preferred_element_type=F32)
            out.append((decs[k], ds, totals[k]))
        return out

    def state_step(c, s_scr, dec, ds, total):
        y_off = _dot(cm_ref[0, chunk_rows(c), :], s_scr[...].astype(BF16)) * dec
        s_scr[...] = s_scr[...] * total + ds
        return y_off

    gf = 4 if nc % 4 == 0 else 1
    gb = gf

    def fwd_group(i, carry):
        chunks = tuple(gf * i + k for k in range(gf))
        cbm, args = [], []
        for c in chunks:
            rows = chunk_rows(c)
            cbm.append(lax.dot_general(cm_ref[0, rows, :], bm_ref[0, rows, :], (((1,), (1,)), ((), ())),
                                       preferred_element_type=F32))
            qside = a_scr[rows, :]
            kside = rhs_scr[pl.ds(pl.multiple_of(c * LANES, LANES), LANES), :]
            args.append([_dot(jnp.where(head_lanes[j], qside, jnp.zeros_like(qside)), kside)
                         for j in range(hpg)])
        ins = state_inputs(chunks, ef, L - 1)
        y_off = [state_step(c, sf_scr, *ins[k]) for k, c in enumerate(chunks)]
        for k, c in enumerate(chunks):
            rows = chunk_rows(c)
            xs = xs_ref[0, rows, :]
            diag = row_scr[pl.ds(pl.multiple_of(c * hpg, hpg), hpg), :]
            pairs = []
            for jp in range(hpg // 2):
                ws = []
                for j in (2 * jp, 2 * jp + 1):
                    arg = jnp.where(below, args[k][j][:, :L],
                                    jnp.where(above, args[k][j][:, L:], diag[j:j + 1, :]))
                    ws.append((cbm[k] * jnp.exp2(arg)).astype(BF16))
                xp = xs[:, jp * LANES:(jp + 1) * LANES]
                zero = jnp.zeros_like(xp)
                rhs = jnp.concatenate([jnp.where(half, xp, zero), jnp.where(half, zero, xp)], axis=0)
                pairs.append(_dot(jnp.concatenate(ws, axis=1), rhs))
            y_scr[rows, :] = jnp.concatenate(pairs, axis=1) + y_off[k]
        return carry

    lax.fori_loop(0, nc // gf, fwd_group, 0)

    dskip = dskip_ref[...]
    gnorm = norm_ref[...]

    def bwd_group(i, carry):
        chunks = tuple(nc - 1 - gb * i - k for k in range(gb))
        ins = state_inputs(chunks, eb, 0)
        for k, c in enumerate(chunks):
            rows = chunk_rows(c)
            y_off = state_step(c, sb_scr, *ins[k])
            xf = xs_ref[0, rows, :].astype(F32)
            y = (y_scr[rows, :] + y_off + dskip * xf) * _silu(z_ref[0, rows, :].astype(F32))
            y = y * lax.rsqrt(jnp.mean(y * y, axis=-1, keepdims=True) + EPS) * gnorm
            o_ref[0, rows, :] = y.astype(o_ref.dtype)
        return carry

    lax.fori_loop(0, nc // gb, bwd_group, 0)


def _dot_split_left(t, a):
    hi = a.astype(BF16)
    lo = (a - hi.astype(F32)).astype(BF16)
    return _dot(t, hi) + _dot(t, lo)


def _ssd(xc, z, dt, bias, alog, dskip, gnorm, ef, eb):
    b, s, _ = xc.shape
    nb = D_INNER // LANES
    grp = lambda n: pl.BlockSpec((1, s, n), lambda i, g: (i, 0, g))
    return pl.pallas_call(
        _ssd_kernel,
        out_shape=jax.ShapeDtypeStruct((b, s, D_INNER), BF16),
        grid=(b, SSM_GROUPS),
        in_specs=[grp(GROUP_DIM),
                  pl.BlockSpec((1, s, D_STATE), lambda i, g: (i, 0, nb + g)),
                  pl.BlockSpec((1, s, D_STATE), lambda i, g: (i, 0, nb + SSM_GROUPS + g)),
                  grp(GROUP_DIM), grp(LANES),
                  pl.BlockSpec((1, 1, LANES), lambda i, g: (g, 0, 0)),
                  pl.BlockSpec((1, 1, LANES), lambda i, g: (g, 0, 0)),
                  pl.BlockSpec((1, GROUP_DIM), lambda i, g: (0, g)),
                  pl.BlockSpec((1, GROUP_DIM), lambda i, g: (0, g)),
                  _const_spec(ef.shape), _const_spec(eb.shape)],
        out_specs=grp(GROUP_DIM),
        scratch_shapes=[pltpu.VMEM((s, GROUP_DIM), F32), pltpu.VMEM((s, LANES), BF16),
                        pltpu.VMEM((s, LANES), F32), pltpu.VMEM((s, LANES), F32),
                        pltpu.VMEM((s // CHUNK * HEADS_PER_GROUP, CHUNK), F32),
                        pltpu.VMEM((s // CHUNK * LANES, 2 * CHUNK), BF16),
                        pltpu.VMEM((D_STATE, GROUP_DIM), F32), pltpu.VMEM((D_STATE, GROUP_DIM), F32)],
        compiler_params=_params(("parallel", "parallel")),
        name="ssd",
    )(xc, xc, xc, z, dt, bias, alog, dskip, gnorm, ef, eb)


def _merge_kernel(x_ref, a_ref, m_ref, ga_ref, gb_ref, pa, pb, wo, o_ref):
    ba = _dot(a_ref[0], pa[...])
    bm = _dot(m_ref[0], pb[...])
    merged = (jax.nn.sigmoid(ga_ref[0].astype(F32)) * ba
              + jax.nn.sigmoid(gb_ref[0].astype(F32)) * bm)
    o_ref[0] = x_ref[0] + _dot(merged.astype(BF16), wo[...])


def _merge(x, a, m, ga, gb, pa, pb, wo, tm):
    b, s, d = x.shape
    tok = lambda n: pl.BlockSpec((1, tm, n), lambda i, j: (i, j, 0))
    return pl.pallas_call(
        _merge_kernel,
        out_shape=jax.ShapeDtypeStruct(x.shape, F32),
        grid=(b, s // tm),
        in_specs=[tok(d), tok(a.shape[2]), tok(m.shape[2]), tok(d), tok(d),
                  _const_spec(pa.shape), _const_spec(pb.shape), _const_spec(wo.shape)],
        out_specs=tok(d),
        compiler_params=_params(("parallel", "parallel")),
        name="merge",
    )(x, a, m, ga, gb, pa, pb, wo)


def _pad_heads(w, width):
    k = w.shape[0]
    w = w.reshape(k, N_HEADS_MLA, width)
    return jnp.pad(w, ((0, 0), (0, 0), (0, HEAD_SLOT - width))).reshape(k, N_HEADS_MLA * HEAD_SLOT)


def _group_lanes(v_f, v_b, fill):
    f = v_f.reshape(SSM_GROUPS, HEADS_PER_GROUP)
    bk = v_b.reshape(SSM_GROUPS, HEADS_PER_GROUP)
    rest = jnp.full((SSM_GROUPS, LANES - 2 * HEADS_PER_GROUP), fill, F32)
    return jnp.concatenate([f, bk, rest], axis=1).reshape(SSM_GROUPS, 1, LANES)


def _expanders():
    ch = np.arange(GROUP_DIM) // SSM_HEAD_DIM
    row = np.arange(LANES)[:, None]
    ef = (row == ch[None, :]).astype(np.float32)
    eb = (row == ch[None, :] + HEADS_PER_GROUP).astype(np.float32)
    return jnp.asarray(ef, BF16), jnp.asarray(eb, BF16)


_ROPE_SWAP = np.concatenate([np.arange(QK_NOPE), np.arange(QK_NOPE + QK_ROPE // 2, QK_HEAD),
                             np.arange(QK_NOPE, QK_NOPE + QK_ROPE // 2), np.arange(QK_HEAD, HEAD_SLOT)])


def _rope_lanes():
    inv_freq = 1.0 / (ROPE_BASE ** (jnp.arange(0, QK_ROPE, 2, dtype=F32) / QK_ROPE))
    half = QK_ROPE // 2
    reps = LANES // QK_ROPE
    invf = jnp.tile(jnp.concatenate([inv_freq, inv_freq]), reps).reshape(1, LANES)
    sgn = jnp.tile(jnp.concatenate([-jnp.ones((half,), F32), jnp.ones((half,), F32)]), reps)
    return invf, sgn.reshape(1, LANES)


def kernel(x, positions, ffn1_norm, ffn1_w_gate, ffn1_w_up, ffn1_w_down, mix_norm, w_in, q_a_norm, w_q_b, kv_a_norm, w_kv_b, q_head_norm, k_head_norm, conv_w, conv_b, a_log_fwd, a_log_bwd, dt_bias_fwd, dt_bias_bwd, d_skip, ssm_norm, w_attn_branch, w_ssm_branch, w_out, ffn2_norm, ffn2_w_gate, ffn2_w_up, ffn2_w_down):
    b, s, d = x.shape
    depth = ffn1_norm.shape[0]
    tm = min(512, s)
    bf = lambda w: w.astype(BF16)
    row = lambda v: v.reshape(1, -1).astype(F32)
    ef, eb = _expanders()
    invf, sgn = _rope_lanes()
    pos = positions.reshape(b, s, 1).astype(jnp.int32)

    for l in range(depth):
        x = _ffn(x, row(ffn1_norm[l]), bf(ffn1_w_gate[l]), bf(ffn1_w_up[l]), bf(ffn1_w_down[l]), tm)

        w = w_in[l]
        o = np.cumsum((0, Q_LORA, KV_LORA, QK_ROPE, D_INNER, XBC_DIM, SSM_HEADS, SSM_HEADS,
                       D_MODEL, D_MODEL))
        seg = [w[:, o[i]:o[i + 1]] for i in range(9)]
        w_kpe = jnp.pad(seg[2], ((0, 0), (QK_NOPE, HEAD_SLOT - QK_HEAD)))
        dtf = seg[5].reshape(d, SSM_GROUPS, HEADS_PER_GROUP)
        dtb = seg[6].reshape(d, SSM_GROUPS, HEADS_PER_GROUP)
        w_dt = jnp.pad(jnp.concatenate([dtf, dtb], axis=2),
                       ((0, 0), (0, 0), (0, LANES - 2 * HEADS_PER_GROUP))).reshape(d, SSM_GROUPS * LANES)
        ws = [bf(seg[0]), bf(seg[1]), bf(w_kpe), bf(seg[3]), bf(seg[4]), bf(w_dt), bf(seg[7]), bf(seg[8])]
        cq, ckv, kpe, z, xbc, dt, ga, gb = _inproj(x, row(mix_norm[l]), ws, tm)

        scale = math.log2(math.e) / math.sqrt(QK_HEAD)
        wkv = w_kv_b[l].reshape(KV_LORA, N_HEADS_MLA, QK_NOPE + V_HEAD)
        w_k = _pad_heads(wkv[:, :, :QK_NOPE].reshape(KV_LORA, N_HEADS_MLA * QK_NOPE), QK_NOPE)
        w_v = wkv[:, :, QK_NOPE:].reshape(KV_LORA, N_HEADS_MLA * V_HEAD)
        pad_gain = lambda g: jnp.pad(g, (0, HEAD_SLOT - QK_HEAD)).reshape(1, HEAD_SLOT)
        w_q = _pad_heads(w_q_b[l], QK_HEAD)
        w_q_sw = w_q.reshape(Q_LORA, N_HEADS_MLA, HEAD_SLOT)[:, :, _ROPE_SWAP].reshape(w_q.shape)
        g_q, g_k = pad_gain(q_head_norm[l] * scale), pad_gain(k_head_norm[l])
        consts = [row(q_a_norm[l]), bf(w_q), bf(w_q_sw), row(kv_a_norm[l]), bf(w_k), bf(w_v),
                  g_q, g_q[:, _ROPE_SWAP], g_k, g_k[:, _ROPE_SWAP], invf, sgn]
        q, k, vt = _mlaprep(cq, ckv, kpe, pos, consts, tm)
        a = _flash(q, k, vt, tq=min(256, s))

        xc = _conv(xbc, conv_w[l].reshape(CONV_WIDTH, XBC_DIM), row(conv_b[l]))
        m = _ssd(xc, z, dt,
                 _group_lanes(dt_bias_fwd[l], dt_bias_bwd[l], 0.0),
                 _group_lanes(a_log_fwd[l], a_log_bwd[l], 0.0),
                 row(jnp.repeat(d_skip[l], SSM_HEAD_DIM)), row(ssm_norm[l]), ef, eb)

        x = _merge(x, a, m, ga, gb, bf(w_attn_branch[l]), bf(w_ssm_branch[l]), bf(w_out[l]), tm)
        x = _ffn(x, row(ffn2_norm[l]), bf(ffn2_w_gate[l]), bf(ffn2_w_up[l]), bf(ffn2_w_down[l]), tm)
    return x
```

```python
import functools
import math

import numpy as np
import jax
import jax.numpy as jnp
from jax import lax
from jax.experimental import pallas as pl
from jax.experimental.pallas import tpu as pltpu

F32 = jnp.float32
BF16 = jnp.bfloat16

D_MODEL = 1024
D_FF = 2816
EPS = 1e-6
N_HEADS_MLA = 16
QK_NOPE = 64
QK_ROPE = 32
QK_HEAD = QK_NOPE + QK_ROPE
V_HEAD = 64
Q_LORA = 384
KV_LORA = 256
ROPE_BASE = 10000.0
D_INNER = 2 * D_MODEL
SSM_HEAD_DIM = 64
SSM_HEADS = D_INNER // SSM_HEAD_DIM
SSM_GROUPS = 4
HEADS_PER_GROUP = SSM_HEADS // SSM_GROUPS
D_STATE = 128
CONV_WIDTH = 5
CHUNK = 128
XBC_DIM = D_INNER + 2 * SSM_GROUPS * D_STATE
GROUP_DIM = D_INNER // SSM_GROUPS

LANES = 128
HEAD_SLOT = LANES
V_SLOT = 2 * V_HEAD
VMEM_LIMIT_BYTES = 56 * 1024 * 1024


def _params(sem, vmem=VMEM_LIMIT_BYTES):
    return pltpu.CompilerParams(dimension_semantics=sem, vmem_limit_bytes=vmem)


def _const_spec(shape):
    nd = len(shape)
    return pl.BlockSpec(shape, lambda *_: (0,) * nd, pipeline_mode=pl.Buffered(1))


def _rms(x, g):
    return x * lax.rsqrt(jnp.mean(x * x, axis=-1, keepdims=True) + EPS) * g


def _silu(x):
    return x * jax.nn.sigmoid(x)


def _dot(a, b):
    return jnp.dot(a, b, preferred_element_type=F32)


def _split3(a):
    hi = a.astype(BF16).astype(F32)
    mid = (a - hi).astype(BF16).astype(F32)
    lo = (a - hi - mid).astype(BF16).astype(F32)
    return hi, mid, lo


def _dot_split(a, b):
    hi = a.astype(BF16)
    lo = (a - hi.astype(F32)).astype(BF16)
    return _dot(hi, b) + _dot(lo, b)


def _ffn_kernel(x_ref, g_ref, wg_ref, wu_ref, wd_ref, o_ref):
    x = x_ref[0]
    hb = _rms(x, g_ref[...]).astype(BF16)
    gate = _dot(hb, wg_ref[...])
    up = _dot(hb, wu_ref[...])
    act = (_silu(gate) * up).astype(BF16)
    o_ref[0] = x + 0.5 * _dot(act, wd_ref[...])


def _ffn(x, g, wg, wu, wd, tm):
    b, s, d = x.shape
    tok = pl.BlockSpec((1, tm, d), lambda i, j: (i, j, 0))
    return pl.pallas_call(
        _ffn_kernel,
        out_shape=jax.ShapeDtypeStruct(x.shape, F32),
        grid=(b, s // tm),
        in_specs=[tok, _const_spec(g.shape), _const_spec(wg.shape), _const_spec(wu.shape),
                  _const_spec(wd.shape)],
        out_specs=tok,
        compiler_params=_params(("parallel", "parallel")),
        name="ffn",
    )(x, g, wg, wu, wd)


def _inproj_kernel(x_ref, g_ref, wcq, wckv, wkpe, wz, wxbc, wdt, wga, wgb,
                   ocq, ockv, okpe, oz, oxbc, odt, oga, ogb):
    hb = _rms(x_ref[0], g_ref[...]).astype(BF16)
    ocq[0] = _dot(hb, wcq[...])
    ockv[0] = _dot(hb, wckv[...])
    okpe[0] = _dot(hb, wkpe[...])
    oz[0] = _dot(hb, wz[...]).astype(BF16)
    oxbc[0] = _dot(hb, wxbc[...]).astype(BF16)
    odt[0] = _dot(hb, wdt[...])
    oga[0] = _dot(hb, wga[...]).astype(BF16)
    ogb[0] = _dot(hb, wgb[...]).astype(BF16)


def _inproj(x, g, ws, tm):
    b, s, d = x.shape
    dts = (F32, F32, F32, BF16, BF16, F32, BF16, BF16)
    tok = lambda n: pl.BlockSpec((1, tm, n), lambda i, j: (i, j, 0))
    return pl.pallas_call(
        _inproj_kernel,
        out_shape=[jax.ShapeDtypeStruct((b, s, w.shape[1]), dt) for w, dt in zip(ws, dts)],
        grid=(b, s // tm),
        in_specs=[tok(d), _const_spec(g.shape)] + [_const_spec(w.shape) for w in ws],
        out_specs=[tok(w.shape[1]) for w in ws],
        compiler_params=_params(("parallel", "parallel")),
        name="inproj",
    )(x, g, *ws)


def _mlaprep_kernel(cq_ref, ckv_ref, kpe_ref, pos_ref, qan, wq, wq_sw, kvan, wk, wv, qhn, qhn_sw, khn, khn_sw,
                    invf, sgn, oq, ok, ov):
    qn = _rms(cq_ref[0], qan[...]).astype(BF16)
    kvn = _rms(ckv_ref[0], kvan[...]).astype(BF16)
    q_full = _dot(qn, wq[...])
    q_part = _dot(qn, wq_sw[...])
    k_full = _dot(kvn, wk[...])
    v_full = _dot(kvn, wv[...])
    ones = jnp.ones((V_SLOT - V_HEAD, v_full.shape[0]), BF16)
    for hp in range(N_HEADS_MLA // 2):
        vt = v_full[:, hp * LANES:(hp + 1) * LANES].T.astype(BF16)
        for i in range(2):
            base = (2 * hp + i) * V_SLOT
            ov[0, 0, base:base + V_HEAD, :] = vt[i * V_HEAD:(i + 1) * V_HEAD, :]
            ov[0, 0, base + V_HEAD:base + V_SLOT, :] = ones
    kpe = kpe_ref[0]
    tm = kpe.shape[0]
    nblk = LANES // QK_ROPE
    quarter = tm // nblk
    pos = pos_ref[0].astype(F32)
    lane_q = lax.broadcasted_iota(jnp.int32, (quarter, LANES), 1)
    pos_p = pos[0:quarter]
    for i in range(1, nblk):
        pos_p = jnp.where(lane_q < i * QK_ROPE, pos_p, pos[i * quarter:(i + 1) * quarter])
    ang = pos_p * invf[...]
    cos_p = jnp.cos(ang)
    sin_p = jnp.sin(ang) * sgn[...]
    rope_lanes = (lane_q >= QK_NOPE) & (lane_q < QK_HEAD)

    def unpack(table, fill):
        parts = []
        for i in range(nblk):
            shift = (QK_NOPE - i * QK_ROPE) % LANES
            moved = pltpu.roll(table, shift, axis=1) if shift else table
            parts.append(jnp.where(rope_lanes, moved, fill))
        return jnp.concatenate(parts, axis=0)

    cos = unpack(cos_p, 1.0)
    sin = unpack(sin_p, 0.0)
    lane = lax.broadcasted_iota(jnp.int32, cos.shape, 1)
    kpe_part = jnp.where(lane < QK_NOPE + QK_ROPE // 2,
                         pltpu.roll(kpe, LANES - QK_ROPE // 2, axis=1),
                         pltpu.roll(kpe, QK_ROPE // 2, axis=1))
    q_cos, q_sin = qhn[...] * cos, qhn_sw[...] * sin
    k_cos, k_sin = khn[...] * cos, khn_sw[...] * sin
    k_rot = kpe_part * k_sin

    def inv_rms(t):
        return lax.rsqrt(jnp.sum(t * t, axis=-1, keepdims=True) * (1.0 / QK_HEAD) + EPS)

    for h in range(N_HEADS_MLA):
        sl = slice(h * HEAD_SLOT, (h + 1) * HEAD_SLOT)
        qh = q_full[:, sl]
        oq[0, :, sl] = ((qh * q_cos + q_part[:, sl] * q_sin) * inv_rms(qh)).astype(BF16)
        kh = k_full[:, sl] + kpe
        ok[0, :, sl] = ((kh * k_cos + k_rot) * inv_rms(kh)).astype(BF16)


def _mlaprep(cq, ckv, kpe, pos, consts, tm):
    b, s, _ = cq.shape
    tok = lambda n: pl.BlockSpec((1, tm, n), lambda i, j: (i, j, 0))
    hw = N_HEADS_MLA * HEAD_SLOT
    vw = N_HEADS_MLA * V_SLOT
    return pl.pallas_call(
        _mlaprep_kernel,
        out_shape=[jax.ShapeDtypeStruct((b, s, hw), BF16), jax.ShapeDtypeStruct((b, s, hw), BF16),
                   jax.ShapeDtypeStruct((b, s // tm, vw, tm), BF16)],
        grid=(b, s // tm),
        in_specs=[tok(Q_LORA), tok(KV_LORA), tok(LANES), tok(1)] + [_const_spec(c.shape) for c in consts],
        out_specs=[tok(hw), tok(hw), pl.BlockSpec((1, 1, vw, tm), lambda i, j: (i, j, 0, 0))],
        compiler_params=_params(("parallel", "parallel")),
        name="mlaprep",
    )(cq, ckv, kpe, pos, *consts)


def _flash_kernel(q_ref, k_ref, vt_ref, o_ref, sta_scr, stb_scr, *, tq):
    nk, _, tk = vt_ref.shape[1:]
    nq = q_ref.shape[1] // tq
    sk = tk
    ns = nk * tk // sk
    heads = (slice(0, HEAD_SLOT), slice(HEAD_SLOT, 2 * HEAD_SLOT))

    def tile_rows(i):
        return pl.ds(pl.multiple_of(i * tq, tq), tq)

    def score_stage(qs, buf, j, mx):
        out = []
        for h, sl in enumerate(heads):
            st = lax.dot_general(k_ref[0, j * sk:(j + 1) * sk, sl], qs[h], (((1,), (1,)), ((), ())),
                                 preferred_element_type=F32)
            buf[h, j * sk:(j + 1) * sk, :] = st
            out.append(jnp.maximum(mx[h], jnp.max(st, axis=0, keepdims=True)))
        return out

    def value_stage(buf, j, mx, accs):
        out = list(accs)
        for t in range(j * sk // tk, (j + 1) * sk // tk):
            for h, sl in enumerate(heads):
                pt = jnp.exp2(buf[h, t * tk:(t + 1) * tk, :] - mx[h]).astype(BF16)
                out[h] = out[h] + _dot(vt_ref[0, t, h * V_SLOT:(h + 1) * V_SLOT, :], pt)
        return out

    def finish(i, accs):
        ot = jnp.concatenate([acc[0:V_HEAD] / acc[V_HEAD:V_HEAD + 1] for acc in accs], axis=0)
        o_ref[0, tile_rows(i), :] = ot.T.astype(o_ref.dtype)

    neg_inf = [jnp.full((1, tq), -jnp.inf, F32)] * 2
    zeros = [jnp.zeros((V_SLOT, tq), F32)] * 2

    def scores_only(i, buf):
        qs = [q_ref[0, tile_rows(i), sl] for sl in heads]
        mx = neg_inf
        for j in range(ns):
            mx = score_stage(qs, buf, j, mx)
        return mx

    def values_only(i, buf, mx):
        accs = zeros
        for j in range(ns):
            accs = value_stage(buf, j, mx, accs)
        finish(i, accs)

    def overlapped(i, buf, mx, buf_next):
        qs = [q_ref[0, tile_rows(i + 1), sl] for sl in heads]
        mx_next, accs = neg_inf, zeros
        for j in range(ns):
            mx_next = score_stage(qs, buf_next, j, mx_next)
            accs = value_stage(buf, j, mx, accs)
        finish(i, accs)
        return mx_next

    def tile_pair(p, mx_a):
        mx_b = overlapped(2 * p, sta_scr, list(mx_a), stb_scr)
        return tuple(overlapped(2 * p + 1, stb_scr, mx_b, sta_scr))

    mx_a = lax.fori_loop(0, nq // 2 - 1, tile_pair, tuple(scores_only(0, sta_scr)))
    mx_b = overlapped(nq - 2, sta_scr, list(mx_a), stb_scr)
    values_only(nq - 1, stb_scr, mx_b)


def _flash(q, k, vt, tq):
    b, s, _ = q.shape
    nk, _, tk = vt.shape[1:]
    return pl.pallas_call(
        functools.partial(_flash_kernel, tq=tq),
        out_shape=jax.ShapeDtypeStruct((b, s, N_HEADS_MLA * V_HEAD), BF16),
        grid=(b, N_HEADS_MLA // 2),
        in_specs=[pl.BlockSpec((1, s, 2 * HEAD_SLOT), lambda i, p: (i, 0, p)),
                  pl.BlockSpec((1, s, 2 * HEAD_SLOT), lambda i, p: (i, 0, p)),
                  pl.BlockSpec((1, nk, 2 * V_SLOT, tk), lambda i, p: (i, 0, p, 0))],
        out_specs=pl.BlockSpec((1, s, 2 * V_HEAD), lambda i, p: (i, 0, p)),
        scratch_shapes=[pltpu.VMEM((2, s, tq), F32), pltpu.VMEM((2, s, tq), F32)],
        compiler_params=_params(("parallel", "parallel")),
        name="flash",
    )(q, k, vt)


CONV_HALO = 16
CONV_WINDOW = 256
CONV_ROWS = CONV_WINDOW - 2 * CONV_HALO


def _conv_kernel(x_ref, w_ref, b_ref, sh_ref, o_ref, pad_ref):
    s, n = x_ref.shape[1:]
    zeros = jnp.zeros((CONV_HALO, n), pad_ref.dtype)
    pad_ref[0:CONV_HALO, :] = zeros
    pad_ref[CONV_HALO + s:2 * CONV_HALO + s, :] = zeros
    pad_ref[CONV_HALO:CONV_HALO + s, :] = x_ref[0]
    w = w_ref[...]
    bias = jnp.broadcast_to(b_ref[...], (CONV_ROWS, n))
    for c in range(-(-s // CONV_ROWS)):
        r0 = min(c * CONV_ROWS, s - CONV_ROWS)
        win = pad_ref[r0:r0 + CONV_WINDOW, :]
        acc = bias
        for t in range(CONV_WIDTH):
            if t == CONV_WIDTH // 2:
                tap = win[CONV_HALO:CONV_HALO + CONV_ROWS, :].astype(F32)
            else:
                tap = _dot(sh_ref[t], win)
            acc = acc + w[t:t + 1, :] * tap
        o_ref[0, r0:r0 + CONV_ROWS, :] = _silu(acc).astype(o_ref.dtype)


def _conv_shifts():
    t = np.arange(CONV_WIDTH)[:, None, None]
    i = np.arange(CONV_ROWS)[None, :, None]
    j = np.arange(CONV_WINDOW)[None, None, :]
    return jnp.asarray(j == CONV_HALO + i + t - CONV_WIDTH // 2, BF16)


def _conv(xbc, w, bias):
    b, s, n = xbc.shape
    sh = _conv_shifts()
    return pl.pallas_call(
        _conv_kernel,
        out_shape=jax.ShapeDtypeStruct(xbc.shape, BF16),
        grid=(b, n // GROUP_DIM),
        in_specs=[pl.BlockSpec((1, s, GROUP_DIM), lambda i, c: (i, 0, c)),
                  pl.BlockSpec((CONV_WIDTH, GROUP_DIM), lambda i, c: (0, c)),
                  pl.BlockSpec((1, GROUP_DIM), lambda i, c: (0, c)),
                  _const_spec(sh.shape)],
        out_specs=pl.BlockSpec((1, s, GROUP_DIM), lambda i, c: (i, 0, c)),
        scratch_shapes=[pltpu.VMEM((s + 2 * CONV_HALO, GROUP_DIM), BF16)],
        compiler_params=_params(("parallel", "parallel")),
        name="conv",
    )(xbc, w, bias, sh)


def _ssd_kernel(xs_ref, bm_ref, cm_ref, z_ref, dt_ref, bias_ref, alog_ref, dskip_ref, norm_ref,
                ef_ref, eb_ref, o_ref, y_scr, a_scr, exp_scr, w_scr, row_scr, rhs_scr, sf_scr, sb_scr):
    L = CHUNK
    hpg = HEADS_PER_GROUP
    nl = 2 * hpg
    nc = xs_ref.shape[1] // L
    a2 = -jnp.exp(alog_ref[0]) * math.log2(math.e)
    bias = bias_ref[0]
    ef = ef_ref[...]
    eb = eb_ref[...]
    li = lax.broadcasted_iota(jnp.int32, (L, L), 0)
    si = lax.broadcasted_iota(jnp.int32, (L, L), 1)
    tril = (si <= li).astype(BF16)
    triu = (si >= li).astype(BF16)
    below = si < li
    above = si > li
    lane = lax.broadcasted_iota(jnp.int32, (L, LANES), 1)
    is_fwd_lane = lane < hpg
    half = lane < SSM_HEAD_DIM
    fwd_rows = lax.broadcasted_iota(jnp.int32, (nl, L), 0) < hpg
    ri = lax.broadcasted_iota(jnp.int32, (3 * nl, 2 * L), 0)
    ci = lax.broadcasted_iota(jnp.int32, (3 * nl, 2 * L), 1)
    ones_rows = ((ri % nl < hpg) == (ci < L)).astype(F32)
    zero_rows = jnp.zeros((nl, 2 * L), F32)
    head_lanes = [lane % hpg == j for j in range(hpg)]

    def prologue(c, carry):
        rows = pl.ds(pl.multiple_of(c * L, L), L)
        dtv = jax.nn.softplus(dt_ref[0, rows, :] + bias)
        da = dtv * a2
        cs = jnp.where(is_fwd_lane, _dot_split_left(tril, da), _dot_split_left(triu, da))
        tot = jnp.where(is_fwd_lane[0:1], cs[L - 1:L, :], cs[0:1, :])
        exp_scr[rows, :] = jnp.exp2(cs)
        w_scr[rows, :] = jnp.exp2(tot - cs) * dtv
        q_hi, q_mid, q_lo = _split3(cs)
        a_scr[rows, :] = jnp.where(
            lane < nl, q_hi, jnp.where(
                lane < 2 * nl, pltpu.roll(q_mid, nl, axis=1), jnp.where(
                    lane < 3 * nl, pltpu.roll(q_lo, 2 * nl, axis=1), jnp.where(
                        (lane >= LANES // 2) & (lane < LANES // 2 + 3 * nl), 1.0, 0.0)))).astype(BF16)
        both = jnp.log2(dtv + pltpu.roll(dtv, LANES - hpg, axis=1))
        packed = jnp.where(lane < nl, cs - jnp.log2(dtv), pltpu.roll(both, nl, axis=1))
        pt = packed.T
        row_scr[pl.ds(pl.multiple_of(c * hpg, hpg), hpg), :] = pt[nl:nl + hpg, :]
        pieces = []
        for piece in _split3(pt[0:nl, :]):
            pieces.append(jnp.concatenate([jnp.where(fwd_rows, -piece, 0.0),
                                           jnp.where(fwd_rows, 0.0, -piece)], axis=1))
        rhs_scr[pl.ds(pl.multiple_of(c * LANES, LANES), LANES), :] = jnp.concatenate(
            [ones_rows, zero_rows] + pieces + [zero_rows], axis=0).astype(BF16)
        return carry

    lax.fori_loop(0, nc, prologue, 0, unroll=4 if nc % 4 == 0 else 1)

    sf_scr[...] = jnp.zeros_like(sf_scr)
    sb_scr[...] = jnp.zeros_like(sb_scr)

    def chunk_rows(c):
        return pl.ds(pl.multiple_of(c * L, L), L)

    def state_inputs(chunks, expand, edge_row):
        decs, wexps, totals = [], [], []
        for c in chunks:
            rows = chunk_rows(c)
            expc = exp_scr[rows, :]
            decs.append(_dot(expc.astype(BF16), expand))
            wexps.append(_dot(w_scr[rows, :].astype(BF16), expand))
            totals.append(_dot_split(expc[edge_row:edge_row + 1, :], expand))
        out = []
        for k, c in enumerate(chunks):
            rows = chunk_rows(c)
            xw = (xs_ref[0, rows, :].astype(F32) * wexps[k]).astype(BF16)
            ds = lax.dot_general(bm_ref[0, rows, :], xw, (((0,), (0,)), ((), ())), preferred_element_type=F32)
            out.append((decs[k], ds, totals[k]))
        return out

    def state_step(c, s_scr, dec, ds, total):
        y_off = _dot(cm_ref[0, chunk_rows(c), :], s_scr[...].astype(BF16)) * dec
        s_scr[...] = s_scr[...] * total + ds
        return y_off

    gf = 4 if nc % 4 == 0 else 1
    gb = gf

    def fwd_group(i, carry):
        chunks = tuple(gf * i + k for k in range(gf))
        cbm, args = [], []
        for c in chunks:
            rows = chunk_rows(c)
            cbm.append(lax.dot_general(cm_ref[0, rows, :], bm_ref[0, rows, :], (((1,), (1,)), ((), ())),
                                       preferred_element_type=F32))
            qside = a_scr[rows, :]
            kside = rhs_scr[pl.ds(pl.multiple_of(c * LANES, LANES), LANES), :]
            args.append([_dot(jnp.where(head_lanes[j], qside, jnp.zeros_like(qside)), kside)
                         for j in range(hpg)])
        ins = state_inputs(chunks, ef, L - 1)
        y_off = [state_step(c, sf_scr, *ins[k]) for k, c in enumerate(chunks)]
        for k, c in enumerate(chunks):
            rows = chunk_rows(c)
            xs = xs_ref[0, rows, :]
            diag = row_scr[pl.ds(pl.multiple_of(c * hpg, hpg), hpg), :]
            pairs = []
            for jp in range(hpg // 2):
                ws = []
                for j in (2 * jp, 2 * jp + 1):
                    arg = jnp.where(below, args[k][j][:, :L],
                                    jnp.where(above, args[k][j][:, L:], diag[j:j + 1, :]))
                    ws.append((cbm[k] * jnp.exp2(arg)).astype(BF16))
                xp = xs[:, jp * LANES:(jp + 1) * LANES]
                zero = jnp.zeros_like(xp)
                rhs = jnp.concatenate([jnp.where(half, xp, zero), jnp.where(half, zero, xp)], axis=0)
                pairs.append(_dot(jnp.concatenate(ws, axis=1), rhs))
            y_scr[rows, :] = jnp.concatenate(pairs, axis=1) + y_off[k]
        return carry

    lax.fori_loop(0, nc // gf, fwd_group, 0)

    dskip = dskip_ref[...]
    gnorm = norm_ref[...]

    def bwd_group(i, carry):
        chunks = tuple(nc - 1 - gb * i - k for k in range(gb))
        ins = state_inputs(chunks, eb, 0)
        for k, c in enumerate(chunks):
            rows = chunk_rows(c)
            y_off = state_step(c, sb_scr, *ins[k])
            xf = xs_ref[0, rows, :].astype(F32)
            y = (y_scr[rows, :] + y_off + dskip * xf) * _silu(z_ref[0, rows, :].astype(F32))
            y = y * lax.rsqrt(jnp.mean(y * y, axis=-1, keepdims=True) + EPS) * gnorm
            o_ref[0, rows, :] = y.astype(o_ref.dtype)
        return carry

    lax.fori_loop(0, nc // gb, bwd_group, 0)


def _dot_split_left(t, a):
    hi = a.astype(BF16)
    lo = (a - hi.astype(F32)).astype(BF16)
    return _dot(t, hi) + _dot(t, lo)


def _ssd(xc, z, dt, bias, alog, dskip, gnorm, ef, eb):
    b, s, _ = xc.shape
    nb = D_INNER // LANES
    grp = lambda n: pl.BlockSpec((1, s, n), lambda i, g: (i, 0, g))
    return pl.pallas_call(
        _ssd_kernel,
        out_shape=jax.ShapeDtypeStruct((b, s, D_INNER), BF16),
        grid=(b, SSM_GROUPS),
        in_specs=[grp(GROUP_DIM),
                  pl.BlockSpec((1, s, D_STATE), lambda i, g: (i, 0, nb + g)),
                  pl.BlockSpec((1, s, D_STATE), lambda i, g: (i, 0, nb + SSM_GROUPS + g)),
                  grp(GROUP_DIM), grp(LANES),
                  pl.BlockSpec((1, 1, LANES), lambda i, g: (g, 0, 0)),
                  pl.BlockSpec((1, 1, LANES), lambda i, g: (g, 0, 0)),
                  pl.BlockSpec((1, GROUP_DIM), lambda i, g: (0, g)),
                  pl.BlockSpec((1, GROUP_DIM), lambda i, g: (0, g)),
                  _const_spec(ef.shape), _const_spec(eb.shape)],
        out_specs=grp(GROUP_DIM),
        scratch_shapes=[pltpu.VMEM((s, GROUP_DIM), F32), pltpu.VMEM((s, LANES), BF16),
                        pltpu.VMEM((s, LANES), F32), pltpu.VMEM((s, LANES), F32),
                        pltpu.VMEM((s // CHUNK * HEADS_PER_GROUP, CHUNK), F32),
                        pltpu.VMEM((s // CHUNK * LANES, 2 * CHUNK), BF16),
                        pltpu.VMEM((D_STATE, GROUP_DIM), F32), pltpu.VMEM((D_STATE, GROUP_DIM), F32)],
        compiler_params=_params(("parallel", "parallel")),
        name="ssd",
    )(xc, xc, xc, z, dt, bias, alog, dskip, gnorm, ef, eb)


def _merge_kernel(x_ref, a_ref, m_ref, ga_ref, gb_ref, pa, pb, wo, o_ref):
    ba = _dot(a_ref[0], pa[...])
    bm = _dot(m_ref[0], pb[...])
    merged = (jax.nn.sigmoid(ga_ref[0].astype(F32)) * ba
              + jax.nn.sigmoid(gb_ref[0].astype(F32)) * bm)
    o_ref[0] = x_ref[0] + _dot(merged.astype(BF16), wo[...])


def _merge(x, a, m, ga, gb, pa, pb, wo, tm):
    b, s, d = x.shape
    tok = lambda n: pl.BlockSpec((1, tm, n), lambda i, j: (i, j, 0))
    return pl.pallas_call(
        _merge_kernel,
        out_shape=jax.ShapeDtypeStruct(x.shape, F32),
        grid=(b, s // tm),
        in_specs=[tok(d), tok(a.shape[2]), tok(m.shape[2]), tok(d), tok(d),
                  _const_spec(pa.shape), _const_spec(pb.shape), _const_spec(wo.shape)],
        out_specs=tok(d),
        compiler_params=_params(("parallel", "parallel")),
        name="merge",
    )(x, a, m, ga, gb, pa, pb, wo)


def _pad_heads(w, width):
    k = w.shape[0]
    w = w.reshape(k, N_HEADS_MLA, width)
    return jnp.pad(w, ((0, 0), (0, 0), (0, HEAD_SLOT - width))).reshape(k, N_HEADS_MLA * HEAD_SLOT)


def _group_lanes(v_f, v_b, fill):
    f = v_f.reshape(SSM_GROUPS, HEADS_PER_GROUP)
    bk = v_b.reshape(SSM_GROUPS, HEADS_PER_GROUP)
    rest = jnp.full((SSM_GROUPS, LANES - 2 * HEADS_PER_GROUP), fill, F32)
    return jnp.concatenate([f, bk, rest], axis=1).reshape(SSM_GROUPS, 1, LANES)


def _expanders():
    ch = np.arange(GROUP_DIM) // SSM_HEAD_DIM
    row = np.arange(LANES)[:, None]
    ef = (row == ch[None, :]).astype(np.float32)
    eb = (row == ch[None, :] + HEADS_PER_GROUP).astype(np.float32)
    return jnp.asarray(ef, BF16), jnp.asarray(eb, BF16)


_ROPE_SWAP = np.concatenate([np.arange(QK_NOPE), np.arange(QK_NOPE + QK_ROPE // 2, QK_HEAD),
                             np.arange(QK_NOPE, QK_NOPE + QK_ROPE // 2), np.arange(QK_HEAD, HEAD_SLOT)])


def _rope_lanes():
    inv_freq = 1.0 / (ROPE_BASE ** (jnp.arange(0, QK_ROPE, 2, dtype=F32) / QK_ROPE))
    half = QK_ROPE // 2
    reps = LANES // QK_ROPE
    invf = jnp.tile(jnp.concatenate([inv_freq, inv_freq]), reps).reshape(1, LANES)
    sgn = jnp.tile(jnp.concatenate([-jnp.ones((half,), F32), jnp.ones((half,), F32)]), reps)
    return invf, sgn.reshape(1, LANES)


def kernel(x, positions, ffn1_norm, ffn1_w_gate, ffn1_w_up, ffn1_w_down, mix_norm, w_in, q_a_norm, w_q_b, kv_a_norm, w_kv_b, q_head_norm, k_head_norm, conv_w, conv_b, a_log_fwd, a_log_bwd, dt_bias_fwd, dt_bias_bwd, d_skip, ssm_norm, w_attn_branch, w_ssm_branch, w_out, ffn2_norm, ffn2_w_gate, ffn2_w_up, ffn2_w_down):
    b, s, d = x.shape
    depth = ffn1_norm.shape[0]
    tm = min(512, s)
    bf = lambda w: w.astype(BF16)
    row = lambda v: v.reshape(1, -1).astype(F32)
    ef, eb = _expanders()
    invf, sgn = _rope_lanes()
    pos = positions.reshape(b, s, 1).astype(jnp.int32)

    for l in range(depth):
        x = _ffn(x, row(ffn1_norm[l]), bf(ffn1_w_gate[l]), bf(ffn1_w_up[l]), bf(ffn1_w_down[l]), tm)

        w = w_in[l]
        o = np.cumsum((0, Q_LORA, KV_LORA, QK_ROPE, D_INNER, XBC_DIM, SSM_HEADS, SSM_HEADS,
                       D_MODEL, D_MODEL))
        seg = [w[:, o[i]:o[i + 1]] for i in range(9)]
        w_kpe = jnp.pad(seg[2], ((0, 0), (QK_NOPE, HEAD_SLOT - QK_HEAD)))
        dtf = seg[5].reshape(d, SSM_GROUPS, HEADS_PER_GROUP)
        dtb = seg[6].reshape(d, SSM_GROUPS, HEADS_PER_GROUP)
        w_dt = jnp.pad(jnp.concatenate([dtf, dtb], axis=2),
                       ((0, 0), (0, 0), (0, LANES - 2 * HEADS_PER_GROUP))).reshape(d, SSM_GROUPS * LANES)
        ws = [bf(seg[0]), bf(seg[1]), bf(w_kpe), bf(seg[3]), bf(seg[4]), bf(w_dt), bf(seg[7]), bf(seg[8])]
        cq, ckv, kpe, z, xbc, dt, ga, gb = _inproj(x, row(mix_norm[l]), ws, tm)

        scale = math.log2(math.e) / math.sqrt(QK_HEAD)
        wkv = w_kv_b[l].reshape(KV_LORA, N_HEADS_MLA, QK_NOPE + V_HEAD)
        w_k = _pad_heads(wkv[:, :, :QK_NOPE].reshape(KV_LORA, N_HEADS_MLA * QK_NOPE), QK_NOPE)
        w_v = wkv[:, :, QK_NOPE:].reshape(KV_LORA, N_HEADS_MLA * V_HEAD)
        pad_gain = lambda g: jnp.pad(g, (0, HEAD_SLOT - QK_HEAD)).reshape(1, HEAD_SLOT)
        w_q = _pad_heads(w_q_b[l], QK_HEAD)
        w_q_sw = w_q.reshape(Q_LORA, N_HEADS_MLA, HEAD_SLOT)[:, :, _ROPE_SWAP].reshape(w_q.shape)
        g_q, g_k = pad_gain(q_head_norm[l] * scale), pad_gain(k_head_norm[l])
        consts = [row(q_a_norm[l]), bf(w_q), bf(w_q_sw), row(kv_a_norm[l]), bf(w_k), bf(w_v),
                  g_q, g_q[:, _ROPE_SWAP], g_k, g_k[:, _ROPE_SWAP], invf, sgn]
        q, k, vt = _mlaprep(cq, ckv, kpe, pos, consts, tm)
        a = _flash(q, k, vt, tq=min(256, s))

        xc = _conv(xbc, conv_w[l].reshape(CONV_WIDTH, XBC_DIM), row(conv_b[l]))
        m = _ssd(xc, z, dt,
                 _group_lanes(dt_bias_fwd[l], dt_bias_bwd[l], 0.0),
                 _group_lanes(a_log_fwd[l], a_log_bwd[l], 0.0),
                 row(jnp.repeat(d_skip[l], SSM_HEAD_DIM)), row(ssm_norm[l]), ef, eb)

        x = _merge(x, a, m, ga, gb, bf(w_attn_branch[l]), bf(w_ssm_branch[l]), bf(w_out[l]), tm)
        x = _ffn(x, row(ffn2_norm[l]), bf(ffn2_w_gate[l]), bf(ffn2_w_up[l]), bf(ffn2_w_down[l]), tm)
    return x
```

```python
import functools
import math

import numpy as np
import jax
import jax.numpy as jnp
from jax import lax
from jax.experimental import pallas as pl
from jax.experimental.pallas import tpu as pltpu

F32 = jnp.float32
BF16 = jnp.bfloat16

D_MODEL = 1024
D_FF = 2816
EPS = 1e-6
N_HEADS_MLA = 16
QK_NOPE = 64
QK_ROPE = 32
QK_HEAD = QK_NOPE + QK_ROPE
V_HEAD = 64
Q_LORA = 384
KV_LORA = 256
ROPE_BASE = 10000.0
D_INNER = 2 * D_MODEL
SSM_HEAD_DIM = 64
SSM_HEADS = D_INNER // SSM_HEAD_DIM
SSM_GROUPS = 4
HEADS_PER_GROUP = SSM_HEADS // SSM_GROUPS
D_STATE = 128
CONV_WIDTH = 5
CHUNK = 128
XBC_DIM = D_INNER + 2 * SSM_GROUPS * D_STATE
GROUP_DIM = D_INNER // SSM_GROUPS

LANES = 128
HEAD_SLOT = LANES
V_SLOT = 2 * V_HEAD
VMEM_LIMIT_BYTES = 56 * 1024 * 1024


def _params(sem, vmem=VMEM_LIMIT_BYTES):
    return pltpu.CompilerParams(dimension_semantics=sem, vmem_limit_bytes=vmem)


def _const_spec(shape):
    nd = len(shape)
    return pl.BlockSpec(shape, lambda *_: (0,) * nd, pipeline_mode=pl.Buffered(1))


def _rms(x, g):
    return x * lax.rsqrt(jnp.mean(x * x, axis=-1, keepdims=True) + EPS) * g


def _silu(x):
    return x * jax.nn.sigmoid(x)


def _dot(a, b):
    return jnp.dot(a, b, preferred_element_type=F32)


def _split3(a):
    hi = a.astype(BF16).astype(F32)
    mid = (a - hi).astype(BF16).astype(F32)
    lo = (a - hi - mid).astype(BF16).astype(F32)
    return hi, mid, lo


def _dot_split(a, b):
    hi = a.astype(BF16)
    lo = (a - hi.astype(F32)).astype(BF16)
    return _dot(hi, b) + _dot(lo, b)


def _ffn_kernel(x_ref, g_ref, wg_ref, wu_ref, wd_ref, o_ref):
    x = x_ref[0]
    hb = _rms(x, g_ref[...]).astype(BF16)
    gate = _dot(hb, wg_ref[...])
    up = _dot(hb, wu_ref[...])
    act = (_silu(gate) * up).astype(BF16)
    o_ref[0] = x + 0.5 * _dot(act, wd_ref[...])


def _ffn(x, g, wg, wu, wd, tm):
    b, s, d = x.shape
    tok = pl.BlockSpec((1, tm, d), lambda i, j: (i, j, 0))
    return pl.pallas_call(
        _ffn_kernel,
        out_shape=jax.ShapeDtypeStruct(x.shape, F32),
        grid=(b, s // tm),
        in_specs=[tok, _const_spec(g.shape), _const_spec(wg.shape), _const_spec(wu.shape),
                  _const_spec(wd.shape)],
        out_specs=tok,
        compiler_params=_params(("parallel", "parallel")),
        name="ffn",
    )(x, g, wg, wu, wd)


def _inproj_kernel(x_ref, g_ref, wcq, wckv, wkpe, wz, wxbc, wdt, wga, wgb,
                   ocq, ockv, okpe, oz, oxbc, odt, oga, ogb):
    hb = _rms(x_ref[0], g_ref[...]).astype(BF16)
    ocq[0] = _dot(hb, wcq[...])
    ockv[0] = _dot(hb, wckv[...])
    okpe[0] = _dot(hb, wkpe[...])
    oz[0] = _dot(hb, wz[...]).astype(BF16)
    oxbc[0] = _dot(hb, wxbc[...]).astype(BF16)
    odt[0] = _dot(hb, wdt[...])
    oga[0] = _dot(hb, wga[...]).astype(BF16)
    ogb[0] = _dot(hb, wgb[...]).astype(BF16)


def _inproj(x, g, ws, tm):
    b, s, d = x.shape
    dts = (F32, F32, F32, BF16, BF16, F32, BF16, BF16)
    tok = lambda n: pl.BlockSpec((1, tm, n), lambda i, j: (i, j, 0))
    return pl.pallas_call(
        _inproj_kernel,
        out_shape=[jax.ShapeDtypeStruct((b, s, w.shape[1]), dt) for w, dt in zip(ws, dts)],
        grid=(b, s // tm),
        in_specs=[tok(d), _const_spec(g.shape)] + [_const_spec(w.shape) for w in ws],
        out_specs=[tok(w.shape[1]) for w in ws],
        compiler_params=_params(("parallel", "parallel")),
        name="inproj",
    )(x, g, *ws)


def _mlaprep_kernel(cq_ref, ckv_ref, kpe_ref, pos_ref, qan, wq, wq_sw, kvan, wk, wv, qhn, qhn_sw, khn, khn_sw,
                    invf, sgn, oq, ok, ov):
    qn = _rms(cq_ref[0], qan[...]).astype(BF16)
    kvn = _rms(ckv_ref[0], kvan[...]).astype(BF16)
    q_full = _dot(qn, wq[...])
    q_part = _dot(qn, wq_sw[...])
    k_full = _dot(kvn, wk[...])
    v_full = _dot(kvn, wv[...])
    ones = jnp.ones((V_SLOT - V_HEAD, v_full.shape[0]), BF16)
    for hp in range(N_HEADS_MLA // 2):
        vt = v_full[:, hp * LANES:(hp + 1) * LANES].T.astype(BF16)
        for i in range(2):
            base = (2 * hp + i) * V_SLOT
            ov[0, 0, base:base + V_HEAD, :] = vt[i * V_HEAD:(i + 1) * V_HEAD, :]
            ov[0, 0, base + V_HEAD:base + V_SLOT, :] = ones
    kpe = kpe_ref[0]
    tm = kpe.shape[0]
    nblk = LANES // QK_ROPE
    quarter = tm // nblk
    pos = pos_ref[0].astype(F32)
    lane_q = lax.broadcasted_iota(jnp.int32, (quarter, LANES), 1)
    pos_p = pos[0:quarter]
    for i in range(1, nblk):
        pos_p = jnp.where(lane_q < i * QK_ROPE, pos_p, pos[i * quarter:(i + 1) * quarter])
    ang = pos_p * invf[...]
    cos_p = jnp.cos(ang)
    sin_p = jnp.sin(ang) * sgn[...]
    rope_lanes = (lane_q >= QK_NOPE) & (lane_q < QK_HEAD)

    def unpack(table, fill):
        parts = []
        for i in range(nblk):
            shift = (QK_NOPE - i * QK_ROPE) % LANES
            moved = pltpu.roll(table, shift, axis=1) if shift else table
            parts.append(jnp.where(rope_lanes, moved, fill))
        return jnp.concatenate(parts, axis=0)

    cos = unpack(cos_p, 1.0)
    sin = unpack(sin_p, 0.0)
    lane = lax.broadcasted_iota(jnp.int32, cos.shape, 1)
    kpe_part = jnp.where(lane < QK_NOPE + QK_ROPE // 2,
                         pltpu.roll(kpe, LANES - QK_ROPE // 2, axis=1),
                         pltpu.roll(kpe, QK_ROPE // 2, axis=1))
    q_cos, q_sin = qhn[...] * cos, qhn_sw[...] * sin
    k_cos, k_sin = khn[...] * cos, khn_sw[...] * sin
    k_rot = kpe_part * k_sin

    def inv_rms(t):
        return lax.rsqrt(jnp.sum(t * t, axis=-1, keepdims=True) * (1.0 / QK_HEAD) + EPS)

    for h in range(N_HEADS_MLA):
        sl = slice(h * HEAD_SLOT, (h + 1) * HEAD_SLOT)
        qh = q_full[:, sl]
        oq[0, :, sl] = ((qh * q_cos + q_part[:, sl] * q_sin) * inv_rms(qh)).astype(BF16)
        kh = k_full[:, sl] + kpe
        ok[0, :, sl] = ((kh * k_cos + k_rot) * inv_rms(kh)).astype(BF16)


def _mlaprep(cq, ckv, kpe, pos, consts, tm):
    b, s, _ = cq.shape
    tok = lambda n: pl.BlockSpec((1, tm, n), lambda i, j: (i, j, 0))
    hw = N_HEADS_MLA * HEAD_SLOT
    vw = N_HEADS_MLA * V_SLOT
    return pl.pallas_call(
        _mlaprep_kernel,
        out_shape=[jax.ShapeDtypeStruct((b, s, hw), BF16), jax.ShapeDtypeStruct((b, s, hw), BF16),
                   jax.ShapeDtypeStruct((b, s // tm, vw, tm), BF16)],
        grid=(b, s // tm),
        in_specs=[tok(Q_LORA), tok(KV_LORA), tok(LANES), tok(1)] + [_const_spec(c.shape) for c in consts],
        out_specs=[tok(hw), tok(hw), pl.BlockSpec((1, 1, vw, tm), lambda i, j: (i, j, 0, 0))],
        compiler_params=_params(("parallel", "parallel")),
        name="mlaprep",
    )(cq, ckv, kpe, pos, *consts)


def _flash_kernel(q_ref, k_ref, vt_ref, o_ref, sta_scr, stb_scr, *, tq):
    nk, _, tk = vt_ref.shape[1:]
    nq = q_ref.shape[1] // tq
    sk = tk
    ns = nk * tk // sk
    heads = (slice(0, HEAD_SLOT), slice(HEAD_SLOT, 2 * HEAD_SLOT))

    def tile_rows(i):
        return pl.ds(pl.multiple_of(i * tq, tq), tq)

    def score_stage(qs, buf, j, mx):
        out = []
        for h, sl in enumerate(heads):
            st = lax.dot_general(k_ref[0, j * sk:(j + 1) * sk, sl], qs[h], (((1,), (1,)), ((), ())),
                                 preferred_element_type=F32)
            buf[h, j * sk:(j + 1) * sk, :] = st
            out.append(jnp.maximum(mx[h], jnp.max(st, axis=0, keepdims=True)))
        return out

    def value_stage(buf, j, mx, accs):
        out = list(accs)
        for t in range(j * sk // tk, (j + 1) * sk // tk):
            for h, sl in enumerate(heads):
                pt = jnp.exp2(buf[h, t * tk:(t + 1) * tk, :] - mx[h]).astype(BF16)
                out[h] = out[h] + _dot(vt_ref[0, t, h * V_SLOT:(h + 1) * V_SLOT, :], pt)
        return out

    def finish(i, accs):
        ot = jnp.concatenate([acc[0:V_HEAD] / acc[V_HEAD:V_HEAD + 1] for acc in accs], axis=0)
        o_ref[0, tile_rows(i), :] = ot.T.astype(o_ref.dtype)

    neg_inf = [jnp.full((1, tq), -jnp.inf, F32)] * 2
    zeros = [jnp.zeros((V_SLOT, tq), F32)] * 2

    def scores_only(i, buf):
        qs = [q_ref[0, tile_rows(i), sl] for sl in heads]
        mx = neg_inf
        for j in range(ns):
            mx = score_stage(qs, buf, j, mx)
        return mx

    def values_only(i, buf, mx):
        accs = zeros
        for j in range(ns):
            accs = value_stage(buf, j, mx, accs)
        finish(i, accs)

    def overlapped(i, buf, mx, buf_next):
        qs = [q_ref[0, tile_rows(i + 1), sl] for sl in heads]
        mx_next, accs = neg_inf, zeros
        for j in range(ns):
            mx_next = score_stage(qs, buf_next, j, mx_next)
            accs = value_stage(buf, j, mx, accs)
        finish(i, accs)
        return mx_next

    def tile_pair(p, mx_a):
        mx_b = overlapped(2 * p, sta_scr, list(mx_a), stb_scr)
        return tuple(overlapped(2 * p + 1, stb_scr, mx_b, sta_scr))

    mx_a = lax.fori_loop(0, nq // 2 - 1, tile_pair, tuple(scores_only(0, sta_scr)))
    mx_b = overlapped(nq - 2, sta_scr, list(mx_a), stb_scr)
    values_only(nq - 1, stb_scr, mx_b)


def _flash(q, k, vt, tq):
    b, s, _ = q.shape
    nk, _, tk = vt.shape[1:]
    return pl.pallas_call(
        functools.partial(_flash_kernel, tq=tq),
        out_shape=jax.ShapeDtypeStruct((b, s, N_HEADS_MLA * V_HEAD), BF16),
        grid=(b, N_HEADS_MLA // 2),
        in_specs=[pl.BlockSpec((1, s, 2 * HEAD_SLOT), lambda i, p: (i, 0, p)),
                  pl.BlockSpec((1, s, 2 * HEAD_SLOT), lambda i, p: (i, 0, p)),
                  pl.BlockSpec((1, nk, 2 * V_SLOT, tk), lambda i, p: (i, 0, p, 0))],
        out_specs=pl.BlockSpec((1, s, 2 * V_HEAD), lambda i, p: (i, 0, p)),
        scratch_shapes=[pltpu.VMEM((2, s, tq), F32), pltpu.VMEM((2, s, tq), F32)],
        compiler_params=_params(("parallel", "parallel")),
        name="flash",
    )(q, k, vt)


CONV_HALO = 16
CONV_WINDOW = 256
CONV_ROWS = CONV_WINDOW - 2 * CONV_HALO


def _conv_kernel(x_ref, w_ref, b_ref, sh_ref, o_ref, pad_ref):
    s, n = x_ref.shape[1:]
    zeros = jnp.zeros((CONV_HALO, n), pad_ref.dtype)
    pad_ref[0:CONV_HALO, :] = zeros
    pad_ref[CONV_HALO + s:2 * CONV_HALO + s, :] = zeros
    pad_ref[CONV_HALO:CONV_HALO + s, :] = x_ref[0]
    w = w_ref[...]
    bias = jnp.broadcast_to(b_ref[...], (CONV_ROWS, n))
    for c in range(-(-s // CONV_ROWS)):
        r0 = min(c * CONV_ROWS, s - CONV_ROWS)
        win = pad_ref[r0:r0 + CONV_WINDOW, :]
        acc = bias
        for t in range(CONV_WIDTH):
            if t == CONV_WIDTH // 2:
                tap = win[CONV_HALO:CONV_HALO + CONV_ROWS, :].astype(F32)
            else:
                tap = _dot(sh_ref[t], win)
            acc = acc + w[t:t + 1, :] * tap
        o_ref[0, r0:r0 + CONV_ROWS, :] = _silu(acc).astype(o_ref.dtype)


def _conv_shifts():
    t = np.arange(CONV_WIDTH)[:, None, None]
    i = np.arange(CONV_ROWS)[None, :, None]
    j = np.arange(CONV_WINDOW)[None, None, :]
    return jnp.asarray(j == CONV_HALO + i + t - CONV_WIDTH // 2, BF16)


def _conv(xbc, w, bias):
    b, s, n = xbc.shape
    sh = _conv_shifts()
    return pl.pallas_call(
        _conv_kernel,
        out_shape=jax.ShapeDtypeStruct(xbc.shape, BF16),
        grid=(b, n // GROUP_DIM),
        in_specs=[pl.BlockSpec((1, s, GROUP_DIM), lambda i, c: (i, 0, c)),
                  pl.BlockSpec((CONV_WIDTH, GROUP_DIM), lambda i, c: (0, c)),
                  pl.BlockSpec((1, GROUP_DIM), lambda i, c: (0, c)),
                  _const_spec(sh.shape)],
        out_specs=pl.BlockSpec((1, s, GROUP_DIM), lambda i, c: (i, 0, c)),
        scratch_shapes=[pltpu.VMEM((s + 2 * CONV_HALO, GROUP_DIM), BF16)],
        compiler_params=_params(("parallel", "parallel")),
        name="conv",
    )(xbc, w, bias, sh)


def _ssd_kernel(xs_ref, bm_ref, cm_ref, z_ref, dt_ref, bias_ref, alog_ref, dskip_ref, norm_ref,
                ef_ref, eb_ref, o_ref, y_scr, a_scr, exp_scr, w_scr, row_scr, rhs_scr, sf_scr, sb_scr):
    L = CHUNK
    hpg = HEADS_PER_GROUP
    nl = 2 * hpg
    nc = xs_ref.shape[1] // L
    a2 = -jnp.exp(alog_ref[0]) * math.log2(math.e)
    bias = bias_ref[0]
    ef = ef_ref[...]
    eb = eb_ref[...]
    li = lax.broadcasted_iota(jnp.int32, (L, L), 0)
    si = lax.broadcasted_iota(jnp.int32, (L, L), 1)
    tril = (si <= li).astype(BF16)
    triu = (si >= li).astype(BF16)
    below = si < li
    above = si > li
    lane = lax.broadcasted_iota(jnp.int32, (L, LANES), 1)
    is_fwd_lane = lane < hpg
    half = lane < SSM_HEAD_DIM
    fwd_rows = lax.broadcasted_iota(jnp.int32, (nl, L), 0) < hpg
    ri = lax.broadcasted_iota(jnp.int32, (3 * nl, 2 * L), 0)
    ci = lax.broadcasted_iota(jnp.int32, (3 * nl, 2 * L), 1)
    ones_rows = ((ri % nl < hpg) == (ci < L)).astype(F32)
    zero_rows = jnp.zeros((nl, 2 * L), F32)
    head_lanes = [lane % hpg == j for j in range(hpg)]

    def prologue(c, carry):
        rows = pl.ds(pl.multiple_of(c * L, L), L)
        dtv = jax.nn.softplus(dt_ref[0, rows, :] + bias)
        da = dtv * a2
        cs = jnp.where(is_fwd_lane, _dot_split_left(tril, da), _dot_split_left(triu, da))
        tot = jnp.where(is_fwd_lane[0:1], cs[L - 1:L, :], cs[0:1, :])
        exp_scr[rows, :] = jnp.exp2(cs)
        w_scr[rows, :] = jnp.exp2(tot - cs) * dtv
        q_hi, q_mid, q_lo = _split3(cs)
        a_scr[rows, :] = jnp.where(
            lane < nl, q_hi, jnp.where(
                lane < 2 * nl, pltpu.roll(q_mid, nl, axis=1), jnp.where(
                    lane < 3 * nl, pltpu.roll(q_lo, 2 * nl, axis=1), jnp.where(
                        (lane >= LANES // 2) & (lane < LANES // 2 + 3 * nl), 1.0, 0.0)))).astype(BF16)
        both = jnp.log2(dtv + pltpu.roll(dtv, LANES - hpg, axis=1))
        packed = jnp.where(lane < nl, cs - jnp.log2(dtv), pltpu.roll(both, nl, axis=1))
        pt = packed.T
        row_scr[pl.ds(pl.multiple_of(c * hpg, hpg), hpg), :] = pt[nl:nl + hpg, :]
        pieces = []
        for piece in _split3(pt[0:nl, :]):
            pieces.append(jnp.concatenate([jnp.where(fwd_rows, -piece, 0.0),
                                           jnp.where(fwd_rows, 0.0, -piece)], axis=1))
        rhs_scr[pl.ds(pl.multiple_of(c * LANES, LANES), LANES), :] = jnp.concatenate(
            [ones_rows, zero_rows] + pieces + [zero_rows], axis=0).astype(BF16)
        return carry

    lax.fori_loop(0, nc, prologue, 0, unroll=4 if nc % 4 == 0 else 1)

    sf_scr[...] = jnp.zeros_like(sf_scr)
    sb_scr[...] = jnp.zeros_like(sb_scr)

    def chunk_rows(c):
        return pl.ds(pl.multiple_of(c * L, L), L)

    def expand_lanes(v, base):
        tiles = []
        for jp in range(hpg // 2):
            lo = jnp.broadcast_to(v[:, base + 2 * jp:base + 2 * jp + 1], (L, LANES))
            hi = jnp.broadcast_to(v[:, base + 2 * jp + 1:base + 2 * jp + 2], (L, LANES))
            tiles.append(jnp.where(half, lo, hi))
        return jnp.concatenate(tiles, axis=1)

    def state_inputs(chunks, expand, edge_row, lane_base=None):
        decs, wexps, totals = [], [], []
        for c in chunks:
            rows = chunk_rows(c)
            expc = exp_scr[rows, :]
            if lane_base is None:
                decs.append(_dot(expc.astype(BF16), expand))
            else:
                decs.append(expand_lanes(expc, lane_base))
            wexps.append(_dot(w_scr[rows, :].astype(BF16), expand))
            totals.append(_dot_split(expc[edge_row:edge_row + 1, :], expand))
        out = []
        for k, c in enumerate(chunks):
            rows = chunk_rows(c)
            xw = (xs_ref[0, rows, :].astype(F32) * wexps[k]).astype(BF16)
            ds = lax.dot_general(bm_ref[0, rows, :], xw, (((0,), (0,)), ((), ())), preferred_element_type=F32)
            out.append((decs[k], ds, totals[k]))
        return out

    def state_step(c, s_scr, dec, ds, total):
        y_off = _dot(cm_ref[0, chunk_rows(c), :], s_scr[...].astype(BF16)) * dec
        s_scr[...] = s_scr[...] * total + ds
        return y_off

    gf = 4 if nc % 4 == 0 else 1
    gb = gf

    def fwd_group(i, carry):
        chunks = tuple(gf * i + k for k in range(gf))
        cbm, args = [], []
        for c in chunks:
            rows = chunk_rows(c)
            cbm.append(lax.dot_general(cm_ref[0, rows, :], bm_ref[0, rows, :], (((1,), (1,)), ((), ())),
                                       preferred_element_type=F32))
            qside = a_scr[rows, :]
            kside = rhs_scr[pl.ds(pl.multiple_of(c * LANES, LANES), LANES), :]
            args.append([_dot(jnp.where(head_lanes[j], qside, jnp.zeros_like(qside)), kside)
                         for j in range(hpg)])
        ins = state_inputs(chunks, ef, L - 1, lane_base=0)
        y_off = [state_step(c, sf_scr, *ins[k]) for k, c in enumerate(chunks)]
        for k, c in enumerate(chunks):
            rows = chunk_rows(c)
            xs = xs_ref[0, rows, :]
            diag = row_scr[pl.ds(pl.multiple_of(c * hpg, hpg), hpg), :]
            pairs = []
            for jp in range(hpg // 2):
                ws = []
                for j in (2 * jp, 2 * jp + 1):
                    arg = jnp.where(below, args[k][j][:, :L],
                                    jnp.where(above, args[k][j][:, L:], diag[j:j + 1, :]))
                    ws.append((cbm[k] * jnp.exp2(arg)).astype(BF16))
                xp = xs[:, jp * LANES:(jp + 1) * LANES]
                zero = jnp.zeros_like(xp)
                rhs = jnp.concatenate([jnp.where(half, xp, zero), jnp.where(half, zero, xp)], axis=0)
                pairs.append(_dot(jnp.concatenate(ws, axis=1), rhs))
            y_scr[rows, :] = jnp.concatenate(pairs, axis=1) + y_off[k]
        return carry

    lax.fori_loop(0, nc // gf, fwd_group, 0)

    dskip = dskip_ref[...]
    gnorm = norm_ref[...]

    def bwd_group(i, carry):
        chunks = tuple(nc - 1 - gb * i - k for k in range(gb))
        ins = state_inputs(chunks, eb, 0)
        for k, c in enumerate(chunks):
            rows = chunk_rows(c)
            y_off = state_step(c, sb_scr, *ins[k])
            xf = xs_ref[0, rows, :].astype(F32)
            y = (y_scr[rows, :] + y_off + dskip * xf) * _silu(z_ref[0, rows, :].astype(F32))
            y = y * lax.rsqrt(jnp.mean(y * y, axis=-1, keepdims=True) + EPS) * gnorm
            o_ref[0, rows, :] = y.astype(o_ref.dtype)
        return carry

    lax.fori_loop(0, nc // gb, bwd_group, 0)


def _dot_split_left(t, a):
    hi = a.astype(BF16)
    lo = (a - hi.astype(F32)).astype(BF16)
    return _dot(t, hi) + _dot(t, lo)


def _ssd(xc, z, dt, bias, alog, dskip, gnorm, ef, eb):
    b, s, _ = xc.shape
    nb = D_INNER // LANES
    grp = lambda n: pl.BlockSpec((1, s, n), lambda i, g: (i, 0, g))
    return pl.pallas_call(
        _ssd_kernel,
        out_shape=jax.ShapeDtypeStruct((b, s, D_INNER), BF16),
        grid=(b, SSM_GROUPS),
        in_specs=[grp(GROUP_DIM),
                  pl.BlockSpec((1, s, D_STATE), lambda i, g: (i, 0, nb + g)),
                  pl.BlockSpec((1, s, D_STATE), lambda i, g: (i, 0, nb + SSM_GROUPS + g)),
                  grp(GROUP_DIM), grp(LANES),
                  pl.BlockSpec((1, 1, LANES), lambda i, g: (g, 0, 0)),
                  pl.BlockSpec((1, 1, LANES), lambda i, g: (g, 0, 0)),
                  pl.BlockSpec((1, GROUP_DIM), lambda i, g: (0, g)),
                  pl.BlockSpec((1, GROUP_DIM), lambda i, g: (0, g)),
                  _const_spec(ef.shape), _const_spec(eb.shape)],
        out_specs=grp(GROUP_DIM),
        scratch_shapes=[pltpu.VMEM((s, GROUP_DIM), F32), pltpu.VMEM((s, LANES), BF16),
                        pltpu.VMEM((s, LANES), F32), pltpu.VMEM((s, LANES), F32),
                        pltpu.VMEM((s // CHUNK * HEADS_PER_GROUP, CHUNK), F32),
                        pltpu.VMEM((s // CHUNK * LANES, 2 * CHUNK), BF16),
                        pltpu.VMEM((D_STATE, GROUP_DIM), F32), pltpu.VMEM((D_STATE, GROUP_DIM), F32)],
        compiler_params=_params(("parallel", "parallel")),
        name="ssd",
    )(xc, xc, xc, z, dt, bias, alog, dskip, gnorm, ef, eb)


def _merge_kernel(x_ref, a_ref, m_ref, ga_ref, gb_ref, pa, pb, wo, o_ref):
    ba = _dot(a_ref[0], pa[...])
    bm = _dot(m_ref[0], pb[...])
    merged = (jax.nn.sigmoid(ga_ref[0].astype(F32)) * ba
              + jax.nn.sigmoid(gb_ref[0].astype(F32)) * bm)
    o_ref[0] = x_ref[0] + _dot(merged.astype(BF16), wo[...])


def _merge(x, a, m, ga, gb, pa, pb, wo, tm):
    b, s, d = x.shape
    tok = lambda n: pl.BlockSpec((1, tm, n), lambda i, j: (i, j, 0))
    return pl.pallas_call(
        _merge_kernel,
        out_shape=jax.ShapeDtypeStruct(x.shape, F32),
        grid=(b, s // tm),
        in_specs=[tok(d), tok(a.shape[2]), tok(m.shape[2]), tok(d), tok(d),
                  _const_spec(pa.shape), _const_spec(pb.shape), _const_spec(wo.shape)],
        out_specs=tok(d),
        compiler_params=_params(("parallel", "parallel")),
        name="merge",
    )(x, a, m, ga, gb, pa, pb, wo)


def _pad_heads(w, width):
    k = w.shape[0]
    w = w.reshape(k, N_HEADS_MLA, width)
    return jnp.pad(w, ((0, 0), (0, 0), (0, HEAD_SLOT - width))).reshape(k, N_HEADS_MLA * HEAD_SLOT)


def _group_lanes(v_f, v_b, fill):
    f = v_f.reshape(SSM_GROUPS, HEADS_PER_GROUP)
    bk = v_b.reshape(SSM_GROUPS, HEADS_PER_GROUP)
    rest = jnp.full((SSM_GROUPS, LANES - 2 * HEADS_PER_GROUP), fill, F32)
    return jnp.concatenate([f, bk, rest], axis=1).reshape(SSM_GROUPS, 1, LANES)


def _expanders():
    ch = np.arange(GROUP_DIM) // SSM_HEAD_DIM
    row = np.arange(LANES)[:, None]
    ef = (row == ch[None, :]).astype(np.float32)
    eb = (row == ch[None, :] + HEADS_PER_GROUP).astype(np.float32)
    return jnp.asarray(ef, BF16), jnp.asarray(eb, BF16)


_ROPE_SWAP = np.concatenate([np.arange(QK_NOPE), np.arange(QK_NOPE + QK_ROPE // 2, QK_HEAD),
                             np.arange(QK_NOPE, QK_NOPE + QK_ROPE // 2), np.arange(QK_HEAD, HEAD_SLOT)])


def _rope_lanes():
    inv_freq = 1.0 / (ROPE_BASE ** (jnp.arange(0, QK_ROPE, 2, dtype=F32) / QK_ROPE))
    half = QK_ROPE // 2
    reps = LANES // QK_ROPE
    invf = jnp.tile(jnp.concatenate([inv_freq, inv_freq]), reps).reshape(1, LANES)
    sgn = jnp.tile(jnp.concatenate([-jnp.ones((half,), F32), jnp.ones((half,), F32)]), reps)
    return invf, sgn.reshape(1, LANES)


def kernel(x, positions, ffn1_norm, ffn1_w_gate, ffn1_w_up, ffn1_w_down, mix_norm, w_in, q_a_norm, w_q_b, kv_a_norm, w_kv_b, q_head_norm, k_head_norm, conv_w, conv_b, a_log_fwd, a_log_bwd, dt_bias_fwd, dt_bias_bwd, d_skip, ssm_norm, w_attn_branch, w_ssm_branch, w_out, ffn2_norm, ffn2_w_gate, ffn2_w_up, ffn2_w_down):
    b, s, d = x.shape
    depth = ffn1_norm.shape[0]
    tm = min(512, s)
    bf = lambda w: w.astype(BF16)
    row = lambda v: v.reshape(1, -1).astype(F32)
    ef, eb = _expanders()
    invf, sgn = _rope_lanes()
    pos = positions.reshape(b, s, 1).astype(jnp.int32)

    for l in range(depth):
        x = _ffn(x, row(ffn1_norm[l]), bf(ffn1_w_gate[l]), bf(ffn1_w_up[l]), bf(ffn1_w_down[l]), tm)

        w = bf(w_in[l])
        o = np.cumsum((0, Q_LORA, KV_LORA, QK_ROPE, D_INNER, XBC_DIM, SSM_HEADS, SSM_HEADS,
                       D_MODEL, D_MODEL))
        seg = [w[:, o[i]:o[i + 1]] for i in range(9)]
        w_kpe = jnp.pad(seg[2], ((0, 0), (QK_NOPE, HEAD_SLOT - QK_HEAD)))
        dtf = seg[5].reshape(d, SSM_GROUPS, HEADS_PER_GROUP)
        dtb = seg[6].reshape(d, SSM_GROUPS, HEADS_PER_GROUP)
        w_dt = jnp.pad(jnp.concatenate([dtf, dtb], axis=2),
                       ((0, 0), (0, 0), (0, LANES - 2 * HEADS_PER_GROUP))).reshape(d, SSM_GROUPS * LANES)
        ws = [seg[0], seg[1], w_kpe, seg[3], seg[4], w_dt, seg[7], seg[8]]
        cq, ckv, kpe, z, xbc, dt, ga, gb = _inproj(x, row(mix_norm[l]), ws, tm)

        scale = math.log2(math.e) / math.sqrt(QK_HEAD)
        wkv = w_kv_b[l].reshape(KV_LORA, N_HEADS_MLA, QK_NOPE + V_HEAD)
        w_k = _pad_heads(wkv[:, :, :QK_NOPE].reshape(KV_LORA, N_HEADS_MLA * QK_NOPE), QK_NOPE)
        w_v = wkv[:, :, QK_NOPE:].reshape(KV_LORA, N_HEADS_MLA * V_HEAD)
        pad_gain = lambda g: jnp.pad(g, (0, HEAD_SLOT - QK_HEAD)).reshape(1, HEAD_SLOT)
        w_q = _pad_heads(w_q_b[l], QK_HEAD)
        w_q_sw = w_q.reshape(Q_LORA, N_HEADS_MLA, HEAD_SLOT)[:, :, _ROPE_SWAP].reshape(w_q.shape)
        g_q, g_k = pad_gain(q_head_norm[l] * scale), pad_gain(k_head_norm[l])
        consts = [row(q_a_norm[l]), bf(w_q), bf(w_q_sw), row(kv_a_norm[l]), bf(w_k), bf(w_v),
                  g_q, g_q[:, _ROPE_SWAP], g_k, g_k[:, _ROPE_SWAP], invf, sgn]
        q, k, vt = _mlaprep(cq, ckv, kpe, pos, consts, tm)
        a = _flash(q, k, vt, tq=min(256, s))

        xc = _conv(xbc, conv_w[l].reshape(CONV_WIDTH, XBC_DIM), row(conv_b[l]))
        m = _ssd(xc, z, dt,
                 _group_lanes(dt_bias_fwd[l], dt_bias_bwd[l], 0.0),
                 _group_lanes(a_log_fwd[l], a_log_bwd[l], 0.0),
                 row(jnp.repeat(d_skip[l], SSM_HEAD_DIM)), row(ssm_norm[l]), ef, eb)

        x = _merge(x, a, m, ga, gb, bf(w_attn_branch[l]), bf(w_ssm_branch[l]), bf(w_out[l]), tm)
        x = _ffn(x, row(ffn2_norm[l]), bf(ffn2_w_gate[l]), bf(ffn2_w_up[l]), bf(ffn2_w_down[l]), tm)
    return x
```

```python
import functools
import math

import numpy as np
import jax
import jax.numpy as jnp
from jax import lax
from jax.experimental import pallas as pl
from jax.experimental.pallas import tpu as pltpu

F32 = jnp.float32
BF16 = jnp.bfloat16

D_MODEL = 1024
D_FF = 2816
EPS = 1e-6
N_HEADS_MLA = 16
QK_NOPE = 64
QK_ROPE = 32
QK_HEAD = QK_NOPE + QK_ROPE
V_HEAD = 64
Q_LORA = 384
KV_LORA = 256
ROPE_BASE = 10000.0
D_INNER = 2 * D_MODEL
SSM_HEAD_DIM = 64
SSM_HEADS = D_INNER // SSM_HEAD_DIM
SSM_GROUPS = 4
HEADS_PER_GROUP = SSM_HEADS // SSM_GROUPS
D_STATE = 128
CONV_WIDTH = 5
CHUNK = 128
XBC_DIM = D_INNER + 2 * SSM_GROUPS * D_STATE
GROUP_DIM = D_INNER // SSM_GROUPS

LANES = 128
HEAD_SLOT = LANES
V_SLOT = 2 * V_HEAD
VMEM_LIMIT_BYTES = 56 * 1024 * 1024


def _params(sem, vmem=VMEM_LIMIT_BYTES):
    return pltpu.CompilerParams(dimension_semantics=sem, vmem_limit_bytes=vmem)


def _const_spec(shape):
    nd = len(shape)
    return pl.BlockSpec(shape, lambda *_: (0,) * nd, pipeline_mode=pl.Buffered(1))


def _rms(x, g):
    return x * lax.rsqrt(jnp.mean(x * x, axis=-1, keepdims=True) + EPS) * g


def _silu(x):
    return x * jax.nn.sigmoid(x)


def _dot(a, b):
    return jnp.dot(a, b, preferred_element_type=F32)


def _split3(a):
    hi = a.astype(BF16).astype(F32)
    mid = (a - hi).astype(BF16).astype(F32)
    lo = (a - hi - mid).astype(BF16).astype(F32)
    return hi, mid, lo


def _dot_split(a, b):
    hi = a.astype(BF16)
    lo = (a - hi.astype(F32)).astype(BF16)
    return _dot(hi, b) + _dot(lo, b)


def _ffn_kernel(x_ref, g_ref, wg_ref, wu_ref, wd_ref, o_ref):
    x = x_ref[0]
    hb = _rms(x, g_ref[...]).astype(BF16)
    gate = _dot(hb, wg_ref[...])
    up = _dot(hb, wu_ref[...])
    act = (_silu(gate) * up).astype(BF16)
    o_ref[0] = x + 0.5 * _dot(act, wd_ref[...])


def _ffn(x, g, wg, wu, wd, tm):
    b, s, d = x.shape
    tok = pl.BlockSpec((1, tm, d), lambda i, j: (i, j, 0))
    return pl.pallas_call(
        _ffn_kernel,
        out_shape=jax.ShapeDtypeStruct(x.shape, F32),
        grid=(b, s // tm),
        in_specs=[tok, _const_spec(g.shape), _const_spec(wg.shape), _const_spec(wu.shape),
                  _const_spec(wd.shape)],
        out_specs=tok,
        compiler_params=_params(("parallel", "parallel")),
        name="ffn",
    )(x, g, wg, wu, wd)


def _inproj_kernel(x_ref, g_ref, wcq, wckv, wkpe, wz, wxbc, wdt, wga, wgb,
                   ocq, ockv, okpe, oz, oxbc, odt, oga, ogb):
    hb = _rms(x_ref[0], g_ref[...]).astype(BF16)
    ocq[0] = _dot(hb, wcq[...])
    ockv[0] = _dot(hb, wckv[...])
    okpe[0] = _dot(hb, wkpe[...])
    oz[0] = _dot(hb, wz[...]).astype(BF16)
    oxbc[0] = _dot(hb, wxbc[...]).astype(BF16)
    odt[0] = _dot(hb, wdt[...])
    oga[0] = _dot(hb, wga[...]).astype(BF16)
    ogb[0] = _dot(hb, wgb[...]).astype(BF16)


def _inproj_mla_kernel(x_ref, g_ref, wcq, wckv, wkpe, wz, wxbc, wdt, wga, wgb, pos_ref, *rest):
    consts, (oz, oxbc, odt, oga, ogb, oq, ok, ov) = rest[:-8], rest[-8:]
    hb = _rms(x_ref[0], g_ref[...]).astype(BF16)
    latents = [(_dot(hb, w[...]),) for w in (wcq, wckv, wkpe)]
    _mlaprep_kernel(*latents, pos_ref, *consts, oq, ok, ov)
    oz[0] = _dot(hb, wz[...]).astype(BF16)
    oxbc[0] = _dot(hb, wxbc[...]).astype(BF16)
    odt[0] = _dot(hb, wdt[...])
    oga[0] = _dot(hb, wga[...]).astype(BF16)
    ogb[0] = _dot(hb, wgb[...]).astype(BF16)


def _inproj_mla(x, g, ws, pos, consts, tm):
    b, s, d = x.shape
    tok = lambda n: pl.BlockSpec((1, tm, n), lambda i, j: (i, j, 0))
    hw, vw = N_HEADS_MLA * HEAD_SLOT, N_HEADS_MLA * V_SLOT
    wide = [(ws[3], BF16), (ws[4], BF16), (ws[5], F32), (ws[6], BF16), (ws[7], BF16)]
    return pl.pallas_call(
        _inproj_mla_kernel,
        out_shape=[jax.ShapeDtypeStruct((b, s, w.shape[1]), dt) for w, dt in wide]
        + [jax.ShapeDtypeStruct((b, s, hw), BF16), jax.ShapeDtypeStruct((b, s, hw), BF16),
           jax.ShapeDtypeStruct((b, s // tm, vw, tm), BF16)],
        grid=(b, s // tm),
        in_specs=[tok(d), _const_spec(g.shape)] + [_const_spec(w.shape) for w in ws] + [tok(1)]
        + [_const_spec(c.shape) for c in consts],
        out_specs=[tok(w.shape[1]) for w, _ in wide]
        + [tok(hw), tok(hw), pl.BlockSpec((1, 1, vw, tm), lambda i, j: (i, j, 0, 0))],
        compiler_params=_params(("parallel", "parallel")),
        name="inproj_mla",
    )(x, g, *ws, pos, *consts)


def _inproj(x, g, ws, tm):
    b, s, d = x.shape
    dts = (F32, F32, F32, BF16, BF16, F32, BF16, BF16)
    tok = lambda n: pl.BlockSpec((1, tm, n), lambda i, j: (i, j, 0))
    return pl.pallas_call(
        _inproj_kernel,
        out_shape=[jax.ShapeDtypeStruct((b, s, w.shape[1]), dt) for w, dt in zip(ws, dts)],
        grid=(b, s // tm),
        in_specs=[tok(d), _const_spec(g.shape)] + [_const_spec(w.shape) for w in ws],
        out_specs=[tok(w.shape[1]) for w in ws],
        compiler_params=_params(("parallel", "parallel")),
        name="inproj",
    )(x, g, *ws)


def _mlaprep_kernel(cq_ref, ckv_ref, kpe_ref, pos_ref, qan, wq, wq_sw, kvan, wk, wv, qhn, qhn_sw, khn, khn_sw,
                    invf, sgn, oq, ok, ov):
    qn = _rms(cq_ref[0], qan[...]).astype(BF16)
    kvn = _rms(ckv_ref[0], kvan[...]).astype(BF16)
    q_full = _dot(qn, wq[...])
    q_part = _dot(qn, wq_sw[...])
    k_full = _dot(kvn, wk[...])
    v_full = _dot(kvn, wv[...])
    ones = jnp.ones((V_SLOT - V_HEAD, v_full.shape[0]), BF16)
    for hp in range(N_HEADS_MLA // 2):
        vt = v_full[:, hp * LANES:(hp + 1) * LANES].T.astype(BF16)
        for i in range(2):
            base = (2 * hp + i) * V_SLOT
            ov[0, 0, base:base + V_HEAD, :] = vt[i * V_HEAD:(i + 1) * V_HEAD, :]
            ov[0, 0, base + V_HEAD:base + V_SLOT, :] = ones
    kpe = kpe_ref[0]
    tm = kpe.shape[0]
    nblk = LANES // QK_ROPE
    quarter = tm // nblk
    pos = pos_ref[0].astype(F32)
    lane_q = lax.broadcasted_iota(jnp.int32, (quarter, LANES), 1)
    pos_p = pos[0:quarter]
    for i in range(1, nblk):
        pos_p = jnp.where(lane_q < i * QK_ROPE, pos_p, pos[i * quarter:(i + 1) * quarter])
    ang = pos_p * invf[...]
    cos_p = jnp.cos(ang)
    sin_p = jnp.sin(ang) * sgn[...]
    rope_lanes = (lane_q >= QK_NOPE) & (lane_q < QK_HEAD)

    def unpack(table, fill):
        parts = []
        for i in range(nblk):
            shift = (QK_NOPE - i * QK_ROPE) % LANES
            moved = pltpu.roll(table, shift, axis=1) if shift else table
            parts.append(jnp.where(rope_lanes, moved, fill))
        return jnp.concatenate(parts, axis=0)

    cos = unpack(cos_p, 1.0)
    sin = unpack(sin_p, 0.0)
    lane = lax.broadcasted_iota(jnp.int32, cos.shape, 1)
    kpe_part = jnp.where(lane < QK_NOPE + QK_ROPE // 2,
                         pltpu.roll(kpe, LANES - QK_ROPE // 2, axis=1),
                         pltpu.roll(kpe, QK_ROPE // 2, axis=1))
    q_cos, q_sin = qhn[...] * cos, qhn_sw[...] * sin
    k_cos, k_sin = khn[...] * cos, khn_sw[...] * sin
    k_rot = kpe_part * k_sin

    def inv_rms(t):
        return lax.rsqrt(jnp.sum(t * t, axis=-1, keepdims=True) * (1.0 / QK_HEAD) + EPS)

    for h in range(N_HEADS_MLA):
        sl = slice(h * HEAD_SLOT, (h + 1) * HEAD_SLOT)
        qh = q_full[:, sl]
        oq[0, :, sl] = ((qh * q_cos + q_part[:, sl] * q_sin) * inv_rms(qh)).astype(BF16)
        kh = k_full[:, sl] + kpe
        ok[0, :, sl] = ((kh * k_cos + k_rot) * inv_rms(kh)).astype(BF16)


def _mlaprep(cq, ckv, kpe, pos, consts, tm):
    b, s, _ = cq.shape
    tok = lambda n: pl.BlockSpec((1, tm, n), lambda i, j: (i, j, 0))
    hw = N_HEADS_MLA * HEAD_SLOT
    vw = N_HEADS_MLA * V_SLOT
    return pl.pallas_call(
        _mlaprep_kernel,
        out_shape=[jax.ShapeDtypeStruct((b, s, hw), BF16), jax.ShapeDtypeStruct((b, s, hw), BF16),
                   jax.ShapeDtypeStruct((b, s // tm, vw, tm), BF16)],
        grid=(b, s // tm),
        in_specs=[tok(Q_LORA), tok(KV_LORA), tok(LANES), tok(1)] + [_const_spec(c.shape) for c in consts],
        out_specs=[tok(hw), tok(hw), pl.BlockSpec((1, 1, vw, tm), lambda i, j: (i, j, 0, 0))],
        compiler_params=_params(("parallel", "parallel")),
        name="mlaprep",
    )(cq, ckv, kpe, pos, *consts)


def _flash_kernel(q_ref, k_ref, vt_ref, o_ref, sta_scr, stb_scr, *, tq):
    nk, _, tk = vt_ref.shape[1:]
    nq = q_ref.shape[1] // tq
    sk = tk
    ns = nk * tk // sk
    heads = (slice(0, HEAD_SLOT), slice(HEAD_SLOT, 2 * HEAD_SLOT))

    def tile_rows(i):
        return pl.ds(pl.multiple_of(i * tq, tq), tq)

    def score_stage(qs, buf, j, mx):
        out = []
        for h, sl in enumerate(heads):
            st = lax.dot_general(k_ref[0, j * sk:(j + 1) * sk, sl], qs[h], (((1,), (1,)), ((), ())),
                                 preferred_element_type=F32)
            buf[h, j * sk:(j + 1) * sk, :] = st
            out.append(jnp.maximum(mx[h], jnp.max(st, axis=0, keepdims=True)))
        return out

    def value_stage(buf, j, mx, accs):
        out = list(accs)
        for t in range(j * sk // tk, (j + 1) * sk // tk):
            for h, sl in enumerate(heads):
                pt = jnp.exp2(buf[h, t * tk:(t + 1) * tk, :] - mx[h]).astype(BF16)
                out[h] = out[h] + _dot(vt_ref[0, t, h * V_SLOT:(h + 1) * V_SLOT, :], pt)
        return out

    def finish(i, accs):
        ot = jnp.concatenate([acc[0:V_HEAD] / acc[V_HEAD:V_HEAD + 1] for acc in accs], axis=0)
        o_ref[0, tile_rows(i), :] = ot.T.astype(o_ref.dtype)

    neg_inf = [jnp.full((1, tq), -jnp.inf, F32)] * 2
    zeros = [jnp.zeros((V_SLOT, tq), F32)] * 2

    def scores_only(i, buf):
        qs = [q_ref[0, tile_rows(i), sl] for sl in heads]
        mx = neg_inf
        for j in range(ns):
            mx = score_stage(qs, buf, j, mx)
        return mx

    def values_only(i, buf, mx):
        accs = zeros
        for j in range(ns):
            accs = value_stage(buf, j, mx, accs)
        finish(i, accs)

    def overlapped(i, buf, mx, buf_next):
        qs = [q_ref[0, tile_rows(i + 1), sl] for sl in heads]
        mx_next, accs = neg_inf, zeros
        for j in range(ns):
            mx_next = score_stage(qs, buf_next, j, mx_next)
            accs = value_stage(buf, j, mx, accs)
        finish(i, accs)
        return mx_next

    def tile_pair(p, mx_a):
        mx_b = overlapped(2 * p, sta_scr, list(mx_a), stb_scr)
        return tuple(overlapped(2 * p + 1, stb_scr, mx_b, sta_scr))

    mx_a = lax.fori_loop(0, nq // 2 - 1, tile_pair, tuple(scores_only(0, sta_scr)))
    mx_b = overlapped(nq - 2, sta_scr, list(mx_a), stb_scr)
    values_only(nq - 1, stb_scr, mx_b)


def _flash(q, k, vt, tq):
    b, s, _ = q.shape
    nk, _, tk = vt.shape[1:]
    return pl.pallas_call(
        functools.partial(_flash_kernel, tq=tq),
        out_shape=jax.ShapeDtypeStruct((b, s, N_HEADS_MLA * V_HEAD), BF16),
        grid=(b, N_HEADS_MLA // 2),
        in_specs=[pl.BlockSpec((1, s, 2 * HEAD_SLOT), lambda i, p: (i, 0, p)),
                  pl.BlockSpec((1, s, 2 * HEAD_SLOT), lambda i, p: (i, 0, p)),
                  pl.BlockSpec((1, nk, 2 * V_SLOT, tk), lambda i, p: (i, 0, p, 0))],
        out_specs=pl.BlockSpec((1, s, 2 * V_HEAD), lambda i, p: (i, 0, p)),
        scratch_shapes=[pltpu.VMEM((2, s, tq), F32), pltpu.VMEM((2, s, tq), F32)],
        compiler_params=_params(("parallel", "parallel")),
        name="flash",
    )(q, k, vt)


CONV_HALO = 16
CONV_WINDOW = 256
CONV_ROWS = CONV_WINDOW - 2 * CONV_HALO


def _conv_kernel(x_ref, w_ref, b_ref, sh_ref, o_ref, pad_ref):
    s, n = x_ref.shape[1:]
    zeros = jnp.zeros((CONV_HALO, n), pad_ref.dtype)
    pad_ref[0:CONV_HALO, :] = zeros
    pad_ref[CONV_HALO + s:2 * CONV_HALO + s, :] = zeros
    pad_ref[CONV_HALO:CONV_HALO + s, :] = x_ref[0]
    w = w_ref[...]
    bias = jnp.broadcast_to(b_ref[...], (CONV_ROWS, n))
    for c in range(-(-s // CONV_ROWS)):
        r0 = min(c * CONV_ROWS, s - CONV_ROWS)
        win = pad_ref[r0:r0 + CONV_WINDOW, :]
        acc = bias
        for t in range(CONV_WIDTH):
            if t == CONV_WIDTH // 2:
                tap = win[CONV_HALO:CONV_HALO + CONV_ROWS, :].astype(F32)
            else:
                tap = _dot(sh_ref[t], win)
            acc = acc + w[t:t + 1, :] * tap
        o_ref[0, r0:r0 + CONV_ROWS, :] = _silu(acc).astype(o_ref.dtype)


def _conv_shifts():
    t = np.arange(CONV_WIDTH)[:, None, None]
    i = np.arange(CONV_ROWS)[None, :, None]
    j = np.arange(CONV_WINDOW)[None, None, :]
    return jnp.asarray(j == CONV_HALO + i + t - CONV_WIDTH // 2, BF16)


def _conv(xbc, w, bias):
    b, s, n = xbc.shape
    sh = _conv_shifts()
    return pl.pallas_call(
        _conv_kernel,
        out_shape=jax.ShapeDtypeStruct(xbc.shape, BF16),
        grid=(b, n // GROUP_DIM),
        in_specs=[pl.BlockSpec((1, s, GROUP_DIM), lambda i, c: (i, 0, c)),
                  pl.BlockSpec((CONV_WIDTH, GROUP_DIM), lambda i, c: (0, c)),
                  pl.BlockSpec((1, GROUP_DIM), lambda i, c: (0, c)),
                  _const_spec(sh.shape)],
        out_specs=pl.BlockSpec((1, s, GROUP_DIM), lambda i, c: (i, 0, c)),
        scratch_shapes=[pltpu.VMEM((s + 2 * CONV_HALO, GROUP_DIM), BF16)],
        compiler_params=_params(("parallel", "parallel")),
        name="conv",
    )(xbc, w, bias, sh)


def _ssd_kernel(xs_ref, bm_ref, cm_ref, z_ref, dt_ref, bias_ref, alog_ref, dskip_ref, norm_ref,
                ef_ref, eb_ref, o_ref, y_scr, a_scr, exp_scr, w_scr, row_scr, rhs_scr, sf_scr, sb_scr):
    L = CHUNK
    hpg = HEADS_PER_GROUP
    nl = 2 * hpg
    nc = xs_ref.shape[1] // L
    a2 = -jnp.exp(alog_ref[0]) * math.log2(math.e)
    bias = bias_ref[0]
    ef = ef_ref[...]
    eb = eb_ref[...]
    li = lax.broadcasted_iota(jnp.int32, (L, L), 0)
    si = lax.broadcasted_iota(jnp.int32, (L, L), 1)
    tril = (si <= li).astype(BF16)
    triu = (si >= li).astype(BF16)
    below = si < li
    above = si > li
    lane = lax.broadcasted_iota(jnp.int32, (L, LANES), 1)
    is_fwd_lane = lane < hpg
    half = lane < SSM_HEAD_DIM
    fwd_rows = lax.broadcasted_iota(jnp.int32, (nl, L), 0) < hpg
    ri = lax.broadcasted_iota(jnp.int32, (3 * nl, 2 * L), 0)
    ci = lax.broadcasted_iota(jnp.int32, (3 * nl, 2 * L), 1)
    ones_rows = ((ri % nl < hpg) == (ci < L)).astype(F32)
    zero_rows = jnp.zeros((nl, 2 * L), F32)
    head_lanes = [lane % hpg == j for j in range(hpg)]

    def prologue(c, carry):
        rows = pl.ds(pl.multiple_of(c * L, L), L)
        dtv = jax.nn.softplus(dt_ref[0, rows, :] + bias)
        da = dtv * a2
        cs = jnp.where(is_fwd_lane, _dot_split_left(tril, da), _dot_split_left(triu, da))
        tot = jnp.where(is_fwd_lane[0:1], cs[L - 1:L, :], cs[0:1, :])
        exp_scr[rows, :] = jnp.exp2(cs)
        w_scr[rows, :] = jnp.exp2(tot - cs) * dtv
        q_hi, q_mid, q_lo = _split3(cs)
        a_scr[rows, :] = jnp.where(
            lane < nl, q_hi, jnp.where(
                lane < 2 * nl, pltpu.roll(q_mid, nl, axis=1), jnp.where(
                    lane < 3 * nl, pltpu.roll(q_lo, 2 * nl, axis=1), jnp.where(
                        (lane >= LANES // 2) & (lane < LANES // 2 + 3 * nl), 1.0, 0.0)))).astype(BF16)
        both = jnp.log2(dtv + pltpu.roll(dtv, LANES - hpg, axis=1))
        packed = jnp.where(lane < nl, cs - jnp.log2(dtv), pltpu.roll(both, nl, axis=1))
        pt = packed.T
        row_scr[pl.ds(pl.multiple_of(c * hpg, hpg), hpg), :] = pt[nl:nl + hpg, :]
        pieces = []
        for piece in _split3(pt[0:nl, :]):
            pieces.append(jnp.concatenate([jnp.where(fwd_rows, -piece, 0.0),
                                           jnp.where(fwd_rows, 0.0, -piece)], axis=1))
        rhs_scr[pl.ds(pl.multiple_of(c * LANES, LANES), LANES), :] = jnp.concatenate(
            [ones_rows, zero_rows] + pieces + [zero_rows], axis=0).astype(BF16)
        return carry

    lax.fori_loop(0, nc, prologue, 0, unroll=4 if nc % 4 == 0 else 1)

    sf_scr[...] = jnp.zeros_like(sf_scr)
    sb_scr[...] = jnp.zeros_like(sb_scr)

    def chunk_rows(c):
        return pl.ds(pl.multiple_of(c * L, L), L)

    def expand_lanes(v, base):
        tiles = []
        for jp in range(hpg // 2):
            lo = jnp.broadcast_to(v[:, base + 2 * jp:base + 2 * jp + 1], (L, LANES))
            hi = jnp.broadcast_to(v[:, base + 2 * jp + 1:base + 2 * jp + 2], (L, LANES))
            tiles.append(jnp.where(half, lo, hi))
        return jnp.concatenate(tiles, axis=1)

    def state_inputs(chunks, expand, edge_row, lane_base=None):
        decs, wexps, totals = [], [], []
        for c in chunks:
            rows = chunk_rows(c)
            expc = exp_scr[rows, :]
            if lane_base is None:
                decs.append(_dot(expc.astype(BF16), expand))
            else:
                decs.append(expand_lanes(expc, lane_base))
            wexps.append(_dot(w_scr[rows, :].astype(BF16), expand))
            totals.append(_dot_split(expc[edge_row:edge_row + 1, :], expand))
        out = []
        for k, c in enumerate(chunks):
            rows = chunk_rows(c)
            xw = (xs_ref[0, rows, :].astype(F32) * wexps[k]).astype(BF16)
            ds = lax.dot_general(bm_ref[0, rows, :], xw, (((0,), (0,)), ((), ())), preferred_element_type=F32)
            out.append((decs[k], ds, totals[k]))
        return out

    def state_step(c, s_scr, dec, ds, total):
        y_off = _dot(cm_ref[0, chunk_rows(c), :], s_scr[...].astype(BF16)) * dec
        s_scr[...] = s_scr[...] * total + ds
        return y_off

    gf = 4 if nc % 4 == 0 else 1
    gb = gf

    def fwd_group(i, carry):
        chunks = tuple(gf * i + k for k in range(gf))
        cbm, args = [], []
        for c in chunks:
            rows = chunk_rows(c)
            cbm.append(lax.dot_general(cm_ref[0, rows, :], bm_ref[0, rows, :], (((1,), (1,)), ((), ())),
                                       preferred_element_type=F32))
            qside = a_scr[rows, :]
            kside = rhs_scr[pl.ds(pl.multiple_of(c * LANES, LANES), LANES), :]
            args.append([_dot(jnp.where(head_lanes[j], qside, jnp.zeros_like(qside)), kside)
                         for j in range(hpg)])
        ins = state_inputs(chunks, ef, L - 1, lane_base=0)
        y_off = [state_step(c, sf_scr, *ins[k]) for k, c in enumerate(chunks)]
        for k, c in enumerate(chunks):
            rows = chunk_rows(c)
            xs = xs_ref[0, rows, :]
            diag = row_scr[pl.ds(pl.multiple_of(c * hpg, hpg), hpg), :]
            pairs = []
            for jp in range(hpg // 2):
                ws = []
                for j in (2 * jp, 2 * jp + 1):
                    arg = jnp.where(below, args[k][j][:, :L],
                                    jnp.where(above, args[k][j][:, L:], diag[j:j + 1, :]))
                    ws.append((cbm[k] * jnp.exp2(arg)).astype(BF16))
                xp = xs[:, jp * LANES:(jp + 1) * LANES]
                zero = jnp.zeros_like(xp)
                rhs = jnp.concatenate([jnp.where(half, xp, zero), jnp.where(half, zero, xp)], axis=0)
                pairs.append(_dot(jnp.concatenate(ws, axis=1), rhs))
            y_scr[rows, :] = jnp.concatenate(pairs, axis=1) + y_off[k]
        return carry

    lax.fori_loop(0, nc // gf, fwd_group, 0)

    dskip = dskip_ref[...]
    gnorm = norm_ref[...]

    def bwd_group(i, carry):
        chunks = tuple(nc - 1 - gb * i - k for k in range(gb))
        ins = state_inputs(chunks, eb, 0)
        for k, c in enumerate(chunks):
            rows = chunk_rows(c)
            y_off = state_step(c, sb_scr, *ins[k])
            xf = xs_ref[0, rows, :].astype(F32)
            y = (y_scr[rows, :] + y_off + dskip * xf) * _silu(z_ref[0, rows, :].astype(F32))
            y = y * lax.rsqrt(jnp.mean(y * y, axis=-1, keepdims=True) + EPS) * gnorm
            o_ref[0, rows, :] = y.astype(o_ref.dtype)
        return carry

    lax.fori_loop(0, nc // gb, bwd_group, 0)


def _dot_split_left(t, a):
    hi = a.astype(BF16)
    lo = (a - hi.astype(F32)).astype(BF16)
    return _dot(t, hi) + _dot(t, lo)


def _ssd(xc, z, dt, bias, alog, dskip, gnorm, ef, eb):
    b, s, _ = xc.shape
    nb = D_INNER // LANES
    grp = lambda n: pl.BlockSpec((1, s, n), lambda i, g: (i, 0, g))
    return pl.pallas_call(
        _ssd_kernel,
        out_shape=jax.ShapeDtypeStruct((b, s, D_INNER), BF16),
        grid=(b, SSM_GROUPS),
        in_specs=[grp(GROUP_DIM),
                  pl.BlockSpec((1, s, D_STATE), lambda i, g: (i, 0, nb + g)),
                  pl.BlockSpec((1, s, D_STATE), lambda i, g: (i, 0, nb + SSM_GROUPS + g)),
                  grp(GROUP_DIM), grp(LANES),
                  pl.BlockSpec((1, 1, LANES), lambda i, g: (g, 0, 0)),
                  pl.BlockSpec((1, 1, LANES), lambda i, g: (g, 0, 0)),
                  pl.BlockSpec((1, GROUP_DIM), lambda i, g: (0, g)),
                  pl.BlockSpec((1, GROUP_DIM), lambda i, g: (0, g)),
                  _const_spec(ef.shape), _const_spec(eb.shape)],
        out_specs=grp(GROUP_DIM),
        scratch_shapes=[pltpu.VMEM((s, GROUP_DIM), F32), pltpu.VMEM((s, LANES), BF16),
                        pltpu.VMEM((s, LANES), F32), pltpu.VMEM((s, LANES), F32),
                        pltpu.VMEM((s // CHUNK * HEADS_PER_GROUP, CHUNK), F32),
                        pltpu.VMEM((s // CHUNK * LANES, 2 * CHUNK), BF16),
                        pltpu.VMEM((D_STATE, GROUP_DIM), F32), pltpu.VMEM((D_STATE, GROUP_DIM), F32)],
        compiler_params=_params(("parallel", "parallel")),
        name="ssd",
    )(xc, xc, xc, z, dt, bias, alog, dskip, gnorm, ef, eb)


def _merge_kernel(x_ref, a_ref, m_ref, ga_ref, gb_ref, pa, pb, wo, o_ref):
    ba = _dot(a_ref[0], pa[...])
    bm = _dot(m_ref[0], pb[...])
    merged = (jax.nn.sigmoid(ga_ref[0].astype(F32)) * ba
              + jax.nn.sigmoid(gb_ref[0].astype(F32)) * bm)
    o_ref[0] = x_ref[0] + _dot(merged.astype(BF16), wo[...])


def _merge(x, a, m, ga, gb, pa, pb, wo, tm):
    b, s, d = x.shape
    tok = lambda n: pl.BlockSpec((1, tm, n), lambda i, j: (i, j, 0))
    return pl.pallas_call(
        _merge_kernel,
        out_shape=jax.ShapeDtypeStruct(x.shape, F32),
        grid=(b, s // tm),
        in_specs=[tok(d), tok(a.shape[2]), tok(m.shape[2]), tok(d), tok(d),
                  _const_spec(pa.shape), _const_spec(pb.shape), _const_spec(wo.shape)],
        out_specs=tok(d),
        compiler_params=_params(("parallel", "parallel")),
        name="merge",
    )(x, a, m, ga, gb, pa, pb, wo)


def _pad_heads(w, width):
    k = w.shape[0]
    w = w.reshape(k, N_HEADS_MLA, width)
    return jnp.pad(w, ((0, 0), (0, 0), (0, HEAD_SLOT - width))).reshape(k, N_HEADS_MLA * HEAD_SLOT)


def _group_lanes(v_f, v_b, fill):
    f = v_f.reshape(SSM_GROUPS, HEADS_PER_GROUP)
    bk = v_b.reshape(SSM_GROUPS, HEADS_PER_GROUP)
    rest = jnp.full((SSM_GROUPS, LANES - 2 * HEADS_PER_GROUP), fill, F32)
    return jnp.concatenate([f, bk, rest], axis=1).reshape(SSM_GROUPS, 1, LANES)


def _expanders():
    ch = np.arange(GROUP_DIM) // SSM_HEAD_DIM
    row = np.arange(LANES)[:, None]
    ef = (row == ch[None, :]).astype(np.float32)
    eb = (row == ch[None, :] + HEADS_PER_GROUP).astype(np.float32)
    return jnp.asarray(ef, BF16), jnp.asarray(eb, BF16)


_ROPE_SWAP = np.concatenate([np.arange(QK_NOPE), np.arange(QK_NOPE + QK_ROPE // 2, QK_HEAD),
                             np.arange(QK_NOPE, QK_NOPE + QK_ROPE // 2), np.arange(QK_HEAD, HEAD_SLOT)])


def _rope_lanes():
    inv_freq = 1.0 / (ROPE_BASE ** (jnp.arange(0, QK_ROPE, 2, dtype=F32) / QK_ROPE))
    half = QK_ROPE // 2
    reps = LANES // QK_ROPE
    invf = jnp.tile(jnp.concatenate([inv_freq, inv_freq]), reps).reshape(1, LANES)
    sgn = jnp.tile(jnp.concatenate([-jnp.ones((half,), F32), jnp.ones((half,), F32)]), reps)
    return invf, sgn.reshape(1, LANES)


def kernel(x, positions, ffn1_norm, ffn1_w_gate, ffn1_w_up, ffn1_w_down, mix_norm, w_in, q_a_norm, w_q_b, kv_a_norm, w_kv_b, q_head_norm, k_head_norm, conv_w, conv_b, a_log_fwd, a_log_bwd, dt_bias_fwd, dt_bias_bwd, d_skip, ssm_norm, w_attn_branch, w_ssm_branch, w_out, ffn2_norm, ffn2_w_gate, ffn2_w_up, ffn2_w_down):
    b, s, d = x.shape
    depth = ffn1_norm.shape[0]
    tm = min(512, s)
    bf = lambda w: w.astype(BF16)
    row = lambda v: v.reshape(1, -1).astype(F32)
    ef, eb = _expanders()
    invf, sgn = _rope_lanes()
    pos = positions.reshape(b, s, 1).astype(jnp.int32)

    for l in range(depth):
        x = _ffn(x, row(ffn1_norm[l]), bf(ffn1_w_gate[l]), bf(ffn1_w_up[l]), bf(ffn1_w_down[l]), tm)

        w = bf(w_in[l])
        o = np.cumsum((0, Q_LORA, KV_LORA, QK_ROPE, D_INNER, XBC_DIM, SSM_HEADS, SSM_HEADS,
                       D_MODEL, D_MODEL))
        seg = [w[:, o[i]:o[i + 1]] for i in range(9)]
        w_kpe = jnp.pad(seg[2], ((0, 0), (QK_NOPE, HEAD_SLOT - QK_HEAD)))
        dtf = seg[5].reshape(d, SSM_GROUPS, HEADS_PER_GROUP)
        dtb = seg[6].reshape(d, SSM_GROUPS, HEADS_PER_GROUP)
        w_dt = jnp.pad(jnp.concatenate([dtf, dtb], axis=2),
                       ((0, 0), (0, 0), (0, LANES - 2 * HEADS_PER_GROUP))).reshape(d, SSM_GROUPS * LANES)
        ws = [seg[0], seg[1], w_kpe, seg[3], seg[4], w_dt, seg[7], seg[8]]
        scale = math.log2(math.e) / math.sqrt(QK_HEAD)
        wkv = w_kv_b[l].reshape(KV_LORA, N_HEADS_MLA, QK_NOPE + V_HEAD)
        w_k = _pad_heads(wkv[:, :, :QK_NOPE].reshape(KV_LORA, N_HEADS_MLA * QK_NOPE), QK_NOPE)
        w_v = wkv[:, :, QK_NOPE:].reshape(KV_LORA, N_HEADS_MLA * V_HEAD)
        pad_gain = lambda g: jnp.pad(g, (0, HEAD_SLOT - QK_HEAD)).reshape(1, HEAD_SLOT)
        w_q = _pad_heads(w_q_b[l], QK_HEAD)
        w_q_sw = w_q.reshape(Q_LORA, N_HEADS_MLA, HEAD_SLOT)[:, :, _ROPE_SWAP].reshape(w_q.shape)
        g_q, g_k = pad_gain(q_head_norm[l] * scale), pad_gain(k_head_norm[l])
        consts = [row(q_a_norm[l]), bf(w_q), bf(w_q_sw), row(kv_a_norm[l]), bf(w_k), bf(w_v),
                  g_q, g_q[:, _ROPE_SWAP], g_k, g_k[:, _ROPE_SWAP], invf, sgn]
        z, xbc, dt, ga, gb, q, k, vt = _inproj_mla(x, row(mix_norm[l]), ws, pos, consts, min(256, s))
        a = _flash(q, k, vt, tq=min(256, s))

        xc = _conv(xbc, conv_w[l].reshape(CONV_WIDTH, XBC_DIM), row(conv_b[l]))
        m = _ssd(xc, z, dt,
                 _group_lanes(dt_bias_fwd[l], dt_bias_bwd[l], 0.0),
                 _group_lanes(a_log_fwd[l], a_log_bwd[l], 0.0),
                 row(jnp.repeat(d_skip[l], SSM_HEAD_DIM)), row(ssm_norm[l]), ef, eb)

        x = _merge(x, a, m, ga, gb, bf(w_attn_branch[l]), bf(w_ssm_branch[l]), bf(w_out[l]), tm)
        x = _ffn(x, row(ffn2_norm[l]), bf(ffn2_w_gate[l]), bf(ffn2_w_up[l]), bf(ffn2_w_down[l]), tm)
    return x
```
